```python
import math
import jax
import jax.numpy as jnp
from jax import lax
import numpy as np

D_MODEL = 1024
BATCH = 32
SEQ = 2048
DEPTH = 2

N_META = 16
RMS_EPS = 1e-6
MASK_VALUE = -1e30
LB_FLOOR = 1e-30

S5_WIDTH = D_MODEL // 4
S5_GROUP_CH = 16
S5_GROUPS = S5_WIDTH // S5_GROUP_CH
S5_STATE = 64
S5_DT_MIN = 1e-3
S5_DT_MAX = 1e-1

HGRN_WIDTH = D_MODEL // 4
HGRN_HEADS = 4
HGRN_HEAD_DIM = HGRN_WIDTH // HGRN_HEADS
HGRN_CHUNK = 64

ATTN_WIDTH = D_MODEL // 2
ATTN_HEADS = 4
ATTN_V_DIM = ATTN_WIDTH // ATTN_HEADS
ATTN_QK_DIM = ATTN_V_DIM // 2
ATTN_BLOCK = 128
ROPE_THETA = 10000.0

MIX_WIDTH = S5_WIDTH + HGRN_WIDTH + ATTN_WIDTH
IN_SECTION_WIDTHS = (S5_WIDTH, HGRN_WIDTH, HGRN_WIDTH, HGRN_WIDTH, HGRN_WIDTH, ATTN_WIDTH, ATTN_WIDTH, ATTN_WIDTH)
IN_WIDTH = sum(IN_SECTION_WIDTHS)

MOE_GROUPS = 4
MOE_EXPERTS_PER_GROUP = 4
MOE_EXPERTS = MOE_GROUPS * MOE_EXPERTS_PER_GROUP
MOE_TOP_K = 2
MOE_FF = 512

kernel_name = 'hymba_s5_hgrn2_diffattn_hmoe'


def rms_norm(x, gain):
    xf = x.astype(jnp.float32)
    y = xf * lax.rsqrt(jnp.mean(xf * xf, axis=-1, keepdims=True) + RMS_EPS)
    return (y * gain.astype(jnp.float32)).astype(x.dtype)


def rotary_tables(length):
    inv_freq = 1.0 / (ROPE_THETA ** (jnp.arange(0, ATTN_QK_DIM, 2, dtype=jnp.float32) / ATTN_QK_DIM))
    ang = jnp.arange(length, dtype=jnp.float32)[:, None] * inv_freq[None, :]
    ang = jnp.concatenate([ang, ang], axis=-1)
    return jnp.cos(ang), jnp.sin(ang)


def apply_rope(t, cos, sin):
    tf = t.astype(jnp.float32)
    t1, t2 = jnp.split(tf, 2, axis=-1)
    rot = jnp.concatenate([-t2, t1], axis=-1)
    c = cos[:, None, None, :]
    s = sin[:, None, None, :]
    return (tf * c + rot * s).astype(t.dtype)


def _cmul(ar, ai, br, bi):
    return ar * br - ai * bi, ar * bi + ai * br


def s5_mixer(u, lam_re, lam_im, log_dt, b_re, b_im, c_re, c_im, d_skip, w_glu):
    f32 = jnp.float32
    bsz, length, _ = u.shape
    uf = u.astype(f32).reshape(bsz, length, S5_GROUPS, S5_GROUP_CH)
    lr = lam_re.astype(f32)
    li = lam_im.astype(f32)
    dt = jnp.exp(log_dt.astype(f32))[:, None]
    mag = jnp.exp(lr * dt)
    abar_r = mag * jnp.cos(li * dt)
    abar_i = mag * jnp.sin(li * dt)
    den = lr * lr + li * li
    zr = abar_r - 1.0
    zi = abar_i
    fr = (zr * lr + zi * li) / den
    fi = (zi * lr - zr * li) / den
    bb_r, bb_i = _cmul(fr[..., None], fi[..., None], b_re.astype(f32), b_im.astype(f32))
    xr = jnp.einsum('blgc,gnc->blgn', uf, bb_r)
    xi = jnp.einsum('blgc,gnc->blgn', uf, bb_i)
    ar = jnp.broadcast_to(abar_r[None, None], (1, length, S5_GROUPS, S5_STATE))
    ai = jnp.broadcast_to(abar_i[None, None], (1, length, S5_GROUPS, S5_STATE))

    def combine(e1, e2):
        a1r, a1i, s1r, s1i = e1
        a2r, a2i, s2r, s2i = e2
        na_r, na_i = _cmul(a2r, a2i, a1r, a1i)
        ns_r, ns_i = _cmul(a2r, a2i, s1r, s1i)
        return na_r, na_i, ns_r + s2r, ns_i + s2i

    _, _, sr, si = lax.associative_scan(combine, (ar, ai, xr, xi), axis=1)
    y = jnp.einsum('blgn,gcn->blgc', sr, c_re.astype(f32)) - jnp.einsum('blgn,gcn->blgc', si, c_im.astype(f32))
    y = y.reshape(bsz, length, S5_WIDTH) + d_skip.astype(f32) * u.astype(f32)
    g = jax.nn.gelu(y)
    gl = g @ w_glu.astype(f32)
    out = gl[..., :S5_WIDTH] * jax.nn.sigmoid(gl[..., S5_WIDTH:])
    return out.astype(u.dtype)


def _gla_chunk(state, q, k, v, log_f):
    c = q.shape[2]
    cum = jnp.cumsum(log_f, axis=2)
    causal = jnp.tril(jnp.ones((c, c), dtype=bool))[None, None, :, :, None]
    rel = cum[:, :, :, None, :] - cum[:, :, None, :, :]
    decay = jnp.where(causal, jnp.exp(jnp.where(causal, rel, 0.0)), 0.0)
    scores = jnp.einsum('bhtd,bhsd,bhtsd->bhts', q, k, decay)
    out = jnp.einsum('bhts,bhsv->bhtv', scores, v) + jnp.einsum('bhtd,bhdv->bhtv', q * jnp.exp(cum), state)
    last = cum[:, :, -1:, :]
    new_state = jnp.exp(last[:, :, 0, :, None]) * state + jnp.einsum('bhsd,bhsv->bhdv', k * jnp.exp(last - cum), v)
    return new_state, out


def hgrn2_mixer(zq, zf, zi, zg, lower_bound, norm_g):
    f32 = jnp.float32
    bsz, length, _ = zq.shape

    def heads(t):
        return t.astype(f32).reshape(bsz, length, HGRN_HEADS, HGRN_HEAD_DIM).transpose(0, 2, 1, 3)

    q = jax.nn.silu(heads(zq))
    v = heads(zi)
    lb = lower_bound.astype(f32).reshape(1, HGRN_HEADS, 1, HGRN_HEAD_DIM)
    log_f = jnp.logaddexp(jnp.log(jnp.maximum(lb, LB_FLOOR)), jnp.log1p(-lb) + jax.nn.log_sigmoid(heads(zf)))
    k = -jnp.expm1(log_f)
    state0 = jnp.zeros((bsz, HGRN_HEADS, HGRN_HEAD_DIM, HGRN_HEAD_DIM), f32)
    state, o_meta = _gla_chunk(state0, q[:, :, :N_META], k[:, :, :N_META], v[:, :, :N_META], log_f[:, :, :N_META])
    n_chunks = (length - N_META) // HGRN_CHUNK

    def to_chunks(t):
        return t[:, :, N_META:].reshape(bsz, HGRN_HEADS, n_chunks, HGRN_CHUNK, -1).transpose(2, 0, 1, 3, 4)

    def step(s, xs):
        return _gla_chunk(s, *xs)

    _, o_rest = lax.scan(step, state, (to_chunks(q), to_chunks(k), to_chunks(v), to_chunks(log_f)))
    o_rest = o_rest.transpose(1, 2, 0, 3, 4).reshape(bsz, HGRN_HEADS, length - N_META, HGRN_HEAD_DIM)
    o = jnp.concatenate([o_meta, o_rest], axis=2).transpose(0, 2, 1, 3)
    o = rms_norm(o, norm_g).reshape(bsz, length, HGRN_WIDTH) * jax.nn.silu(zg.astype(f32))
    return o.astype(zq.dtype)


def diff_attention(zq, zk, zv, lq1, lk1, lq2, lk2, subln_g, lam_init, cos, sin):
    f32 = jnp.float32
    bsz, length, _ = zq.shape
    q = apply_rope(zq.reshape(bsz, length, ATTN_HEADS, 2, ATTN_QK_DIM), cos, sin)
    k = apply_rope(zk.reshape(bsz, length, ATTN_HEADS, 2, ATTN_QK_DIM), cos, sin)
    v = zv.reshape(bsz, length, ATTN_HEADS, ATTN_V_DIM)
    lam = (jnp.exp(jnp.sum(lq1.astype(f32) * lk1.astype(f32)))
           - jnp.exp(jnp.sum(lq2.astype(f32) * lk2.astype(f32))) + lam_init)
    scale = ATTN_QK_DIM ** -0.5
    bounds = [(0, N_META)] + [(s, s + ATTN_BLOCK) for s in range(N_META, length, ATTN_BLOCK)]
    outs = []
    for qs, qe in bounds:
        s = jnp.einsum('bqhcd,bkhcd->bhcqk', q[:, qs:qe], k[:, :qe]).astype(f32) * scale
        causal = jnp.arange(qe)[None, :] <= jnp.arange(qs, qe)[:, None]
        p = jax.nn.softmax(jnp.where(causal, s, MASK_VALUE), axis=-1)
        a = p[:, :, 0] - lam * p[:, :, 1]
        outs.append(jnp.einsum('bhqk,bkhv->bqhv', a.astype(v.dtype), v[:, :qe]))
    o = jnp.concatenate(outs, axis=1)
    o = rms_norm(o, subln_g) * (1.0 - lam_init)
    return o.reshape(bsz, length, ATTN_WIDTH)


def hier_moe(h, w_group, b_group, w_expert, b_expert, w_gate, w_up, w_down):
    f32 = jnp.float32
    bsz, length, d = h.shape
    t = h.reshape(-1, d)
    group_prob = jax.nn.softmax((t @ w_group).astype(f32) + b_group.astype(f32), axis=-1)
    gp, gi = lax.top_k(group_prob, 1)
    exp_logits = ((t @ w_expert).astype(f32) + b_expert.astype(f32)).reshape(-1, MOE_GROUPS, MOE_EXPERTS_PER_GROUP)
    in_group = jnp.einsum('tg,tge->te', jax.nn.one_hot(gi[:, 0], MOE_GROUPS, dtype=f32), exp_logits)
    ev, ei = lax.top_k(in_group, MOE_TOP_K)
    ew = jax.nn.softmax(ev, axis=-1) * gp
    expert_id = gi * MOE_EXPERTS_PER_GROUP + ei
    gates = jnp.einsum('tke,tk->te', jax.nn.one_hot(expert_id, MOE_EXPERTS, dtype=f32), ew).astype(t.dtype)
    y = jnp.zeros_like(t)
    for e in range(MOE_EXPERTS):
        he = jax.nn.silu(t @ w_gate[e]) * (t @ w_up[e])
        y = y + gates[:, e:e + 1] * (he @ w_down[e])
    return y.reshape(bsz, length, d)


def setup_inputs(seed: int = 0) -> dict:
    key = jax.random.key(seed)
    ks = jax.random.split(key, 32)
    f32 = jnp.float32

    def nrm(k, shape, scale):
        return scale * jax.random.normal(k, shape, f32)

    n_idx = jnp.arange(S5_STATE, dtype=f32)
    return {
        'x': nrm(ks[0], (BATCH, SEQ, D_MODEL), 1.0),
        'meta_tokens': nrm(ks[1], (N_META, D_MODEL), 1.0),
        'norm_mix_g': 1.0 + nrm(ks[2], (DEPTH, D_MODEL), 0.02),
        'w_in': nrm(ks[3], (DEPTH, D_MODEL, IN_WIDTH), D_MODEL ** -0.5),
        's5_lambda_re': -0.5 + nrm(ks[4], (DEPTH, S5_GROUPS, S5_STATE), 0.01),
        's5_lambda_im': math.pi * n_idx + nrm(ks[5], (DEPTH, S5_GROUPS, S5_STATE), 0.01),
        's5_log_dt': jax.random.uniform(ks[6], (DEPTH, S5_GROUPS), f32, math.log(S5_DT_MIN), math.log(S5_DT_MAX)),
        's5_b_re': nrm(ks[7], (DEPTH, S5_GROUPS, S5_STATE, S5_GROUP_CH), (2 * S5_GROUP_CH) ** -0.5),
        's5_b_im': nrm(ks[8], (DEPTH, S5_GROUPS, S5_STATE, S5_GROUP_CH), (2 * S5_GROUP_CH) ** -0.5),
        's5_c_re': nrm(ks[9], (DEPTH, S5_GROUPS, S5_GROUP_CH, S5_STATE), S5_STATE ** -0.5),
        's5_c_im': nrm(ks[10], (DEPTH, S5_GROUPS, S5_GROUP_CH, S5_STATE), S5_STATE ** -0.5),
        's5_d': nrm(ks[11], (DEPTH, S5_WIDTH), 1.0),
        's5_w_glu': nrm(ks[12], (DEPTH, S5_WIDTH, 2 * S5_WIDTH), S5_WIDTH ** -0.5),
        'hgrn_lower_bounds': nrm(ks[13], (DEPTH, HGRN_WIDTH), 0.1),
        'hgrn_norm_g': 1.0 + nrm(ks[14], (DEPTH, HGRN_HEAD_DIM), 0.02),
        'diff_lambda_q1': nrm(ks[15], (DEPTH, ATTN_QK_DIM), 0.1),
        'diff_lambda_k1': nrm(ks[16], (DEPTH, ATTN_QK_DIM), 0.1),
        'diff_lambda_q2': nrm(ks[17], (DEPTH, ATTN_QK_DIM), 0.1),
        'diff_lambda_k2': nrm(ks[18], (DEPTH, ATTN_QK_DIM), 0.1),
        'diff_subln_g': 1.0 + nrm(ks[19], (DEPTH, ATTN_V_DIM), 0.02),
        'w_out': nrm(ks[20], (DEPTH, MIX_WIDTH, D_MODEL), MIX_WIDTH ** -0.5),
        'norm_ffn_g': 1.0 + nrm(ks[21], (DEPTH, D_MODEL), 0.02),
        'moe_w_group': nrm(ks[22], (DEPTH, D_MODEL, MOE_GROUPS), D_MODEL ** -0.5),
        'moe_b_group': nrm(ks[23], (DEPTH, MOE_GROUPS), 0.01),
        'moe_w_expert': nrm(ks[24], (DEPTH, D_MODEL, MOE_EXPERTS), D_MODEL ** -0.5),
        'moe_b_expert': nrm(ks[25], (DEPTH, MOE_EXPERTS), 0.01),
        'moe_w_gate': nrm(ks[26], (DEPTH, MOE_EXPERTS, D_MODEL, MOE_FF), D_MODEL ** -0.5),
        'moe_w_up': nrm(ks[27], (DEPTH, MOE_EXPERTS, D_MODEL, MOE_FF), D_MODEL ** -0.5),
        'moe_w_down': nrm(ks[28], (DEPTH, MOE_EXPERTS, MOE_FF, D_MODEL), MOE_FF ** -0.5),
        'final_norm_g': 1.0 + nrm(ks[29], (D_MODEL,), 0.02),
    }


def reference(x, meta_tokens, norm_mix_g, w_in, s5_lambda_re, s5_lambda_im, s5_log_dt, s5_b_re, s5_b_im,
              s5_c_re, s5_c_im, s5_d, s5_w_glu, hgrn_lower_bounds, hgrn_norm_g, diff_lambda_q1, diff_lambda_k1,
              diff_lambda_q2, diff_lambda_k2, diff_subln_g, w_out, norm_ffn_g, moe_w_group, moe_b_group,
              moe_w_expert, moe_b_expert, moe_w_gate, moe_w_up, moe_w_down, final_norm_g):
    bsz = x.shape[0]
    meta = jnp.broadcast_to(meta_tokens.astype(x.dtype)[None], (bsz, N_META, D_MODEL))
    h = jnp.concatenate([meta, x], axis=1)
    length = h.shape[1]
    cos, sin = rotary_tables(length)
    lb_w = jax.nn.softmax(hgrn_lower_bounds.astype(jnp.float32), axis=0)
    lower_bounds = jnp.cumsum(lb_w, axis=0) - lb_w[0:1]
    split_points = [int(p) for p in np.cumsum(IN_SECTION_WIDTHS)[:-1]]
    for layer in range(DEPTH):
        lam_init = 0.8 - 0.6 * math.exp(-0.3 * layer)
        proj = rms_norm(h, norm_mix_g[layer]) @ w_in[layer]
        u, hq, hf, hi, hg, aq, ak, av = jnp.split(proj, split_points, axis=-1)
        y_ssm = s5_mixer(u, s5_lambda_re[layer], s5_lambda_im[layer], s5_log_dt[layer], s5_b_re[layer],
                         s5_b_im[layer], s5_c_re[layer], s5_c_im[layer], s5_d[layer], s5_w_glu[layer])
        y_rec = hgrn2_mixer(hq, hf, hi, hg, lower_bounds[layer], hgrn_norm_g[layer])
        y_att = diff_attention(aq, ak, av, diff_lambda_q1[layer], diff_lambda_k1[layer], diff_lambda_q2[layer],
                               diff_lambda_k2[layer], diff_subln_g[layer], lam_init, cos, sin)
        mixed = jnp.concatenate([y_ssm.astype(h.dtype), y_rec.astype(h.dtype), y_att.astype(h.dtype)], axis=-1)
        h = h + mixed @ w_out[layer]
        h = h + hier_moe(rms_norm(h, norm_ffn_g[layer]), moe_w_group[layer], moe_b_group[layer],
                         moe_w_expert[layer], moe_b_expert[layer], moe_w_gate[layer], moe_w_up[layer],
                         moe_w_down[layer])
    h = rms_norm(h, final_norm_g)
    return h[:, N_META:]
```

```python
import functools
import math

import numpy as np
import jax
import jax.numpy as jnp
from jax import lax
from jax.experimental import pallas as pl
from jax.experimental.pallas import tpu as pltpu

F32 = jnp.float32
BF16 = jnp.bfloat16

N_META = 16
PAD = 112
RMS_EPS = 1e-6
MASK_VALUE = -1e30
LB_FLOOR = 1e-30
ROPE_THETA = 10000.0

S5_GROUP_CH = 16
S5_STATE = 64
HGRN_HEADS = 4
HGRN_CHUNK = 64
ATTN_HEADS = 4
MOE_GROUPS = 4
MOE_EXPERTS_PER_GROUP = 4
MOE_EXPERTS = MOE_GROUPS * MOE_EXPERTS_PER_GROUP
ROUTER_LANES = 128

VMEM_LIMIT = 56 * 1024 * 1024


def _params(sem, vmem=VMEM_LIMIT):
    return pltpu.CompilerParams(dimension_semantics=sem, vmem_limit_bytes=vmem)


def _row_tile(n, cap, mult=16):
    best = None
    for t in range(mult, min(n, cap) + 1, mult):
        if n % t == 0:
            best = t
    assert best is not None, (n, cap)
    return best


def _sigmoid(x):
    return 1.0 / (1.0 + jnp.exp(-x))


def _dot(a, b):
    return jnp.dot(a, b, preferred_element_type=F32)


def _dot_nt(a, b):
    return lax.dot_general(a, b, (((1,), (1,)), ((), ())), preferred_element_type=F32)


def _dot_tn(a, b):
    return lax.dot_general(a, b, (((0,), (0,)), ((), ())), preferred_element_type=F32)


def _split2(x):
    hi = x.astype(BF16)
    lo = (x - hi.astype(F32)).astype(BF16)
    return hi, lo


def _split3(x):
    hi = x.astype(BF16)
    r = x - hi.astype(F32)
    mid = r.astype(BF16)
    lo = (r - mid.astype(F32)).astype(BF16)
    return hi, mid, lo


def _in_proj_kernel(h_ref, g_ref, w_ref, cos_ref, sina_ref, sinb_ref,
                    u_ref, hq_ref, hf_ref, hi_ref, hg_ref, q_ref, k1_ref, k2_ref, v_ref):
    x = h_ref[0]
    ms = jnp.mean(x * x, axis=-1, keepdims=True)
    xn = (x * lax.rsqrt(ms + RMS_EPS) * g_ref[...]).astype(BF16)

    def sec(lo, hi):
        return _dot(xn, w_ref[:, lo:hi])

    u_ref[...] = sec(0, 256)
    hq_ref[0] = sec(256, 512)
    hf_ref[0] = sec(512, 768)
    hi_ref[0] = sec(768, 1024)
    hg_ref[0] = sec(1024, 1280)

    cos = cos_ref[...]
    sina = sina_ref[...]
    sinb = sinb_ref[...]
    lane = lax.broadcasted_iota(jnp.int32, cos.shape, 1)
    first = lane < 64

    def rope(t):
        return t * cos + pltpu.roll(t, 96, 1) * sina + pltpu.roll(t, 32, 1) * sinb

    for hd in range(ATTN_HEADS):
        c0 = 1280 + hd * 128
        q = rope(sec(c0, c0 + 128)) * 0.125
        q_ref[0, :, hd * 128:(hd + 1) * 128] = q.astype(BF16)
        c0 = 1792 + hd * 128
        k = rope(sec(c0, c0 + 128))
        k1_ref[0, :, hd * 128:(hd + 1) * 128] = jnp.where(first, k, 0.0).astype(BF16)
        k2_ref[0, :, hd * 128:(hd + 1) * 128] = jnp.where(first, 0.0, k).astype(BF16)
    v_ref[0] = sec(2304, 2816).astype(BF16)


def _in_proj(h, g, w_bf, cos, sina, sinb):
    bsz, lp, d = h.shape
    tl = _row_tile(lp, 544)
    nl = lp // tl
    row = lambda b, i: (b, i, 0)
    tab = pl.BlockSpec((tl, 128), lambda b, i: (i, 0))
    out_shape = (
        jax.ShapeDtypeStruct((lp, bsz * 256), F32),
        jax.ShapeDtypeStruct((bsz, lp, 256), F32),
        jax.ShapeDtypeStruct((bsz, lp, 256), F32),
        jax.ShapeDtypeStruct((bsz, lp, 256), F32),
        jax.ShapeDtypeStruct((bsz, lp, 256), F32),
        jax.ShapeDtypeStruct((bsz, lp, 512), BF16),
        jax.ShapeDtypeStruct((bsz, lp, 512), BF16),
        jax.ShapeDtypeStruct((bsz, lp, 512), BF16),
        jax.ShapeDtypeStruct((bsz, lp, 512), BF16),
    )
    s256 = pl.BlockSpec((1, tl, 256), row)
    s512 = pl.BlockSpec((1, tl, 512), row)
    return pl.pallas_call(
        _in_proj_kernel,
        grid=(bsz, nl),
        in_specs=[pl.BlockSpec((1, tl, d), row),
                  pl.BlockSpec((1, d), lambda b, i: (0, 0)),
                  pl.BlockSpec(w_bf.shape, lambda b, i: (0, 0)),
                  tab, tab, tab],
        out_specs=(pl.BlockSpec((tl, 256), lambda b, i: (i, b)), s256, s256, s256, s256,
                   s512, s512, s512, s512),
        out_shape=out_shape,
        compiler_params=_params(("parallel", "parallel")),
        name="in_proj",
    )(h, g, w_bf, cos, sina, sinb)


def _s5_kernel(u_ref, bbar_ref, a_ref, cmat_ref, d_ref, wglu_ref, y_ref, xs_ref, st_ref):
    tt = u_ref.shape[0]
    nst = a_ref.shape[1] // 2

    @pl.when(pl.program_id(1) == 0)
    def _():
        st_ref[...] = jnp.zeros_like(st_ref)

    u2 = u_ref[...].reshape(tt * 8, u_ref.shape[2])
    xs_ref[...] = _dot(u2.astype(BF16), bbar_ref[...])

    ar = a_ref[:, :nst]
    ai = a_ref[:, nst:]

    def step(t, carry):
        sr, si = carry
        r0 = pl.multiple_of(t * 8, 8)
        xr = xs_ref[pl.ds(r0, 8), :nst]
        xi = xs_ref[pl.ds(r0, 8), nst:]
        nr = ar * sr - ai * si + xr
        ni = ar * si + ai * sr + xi
        xs_ref[pl.ds(r0, 8), :nst] = nr
        xs_ref[pl.ds(r0, 8), nst:] = ni
        return nr, ni

    sr, si = lax.fori_loop(0, tt, step, (st_ref[:, :nst], st_ref[:, nst:]))
    st_ref[:, :nst] = sr
    st_ref[:, nst:] = si

    y = _dot(xs_ref[...].astype(BF16), cmat_ref[...]) + d_ref[...] * u2
    g = 0.5 * y * (1.0 + jnp.tanh(0.7978845608028654 * (y + 0.044715 * (y * y * y))))
    gl = _dot(g.astype(BF16), wglu_ref[...])
    w = gl.shape[1] // 2
    out = gl[:, :w] * _sigmoid(gl[:, w:])
    y_ref[...] = out.reshape(y_ref.shape)


def _s5(u3, bbar, a8, cmat, dskip, wglu):
    lp, bsz, w = u3.shape
    tt = 128
    nst2 = a8.shape[1]
    const = lambda bg, i: (0, 0)
    return pl.pallas_call(
        _s5_kernel,
        grid=(bsz // 8, lp // tt),
        in_specs=[pl.BlockSpec((tt, 8, w), lambda bg, i: (i, bg, 0)),
                  pl.BlockSpec(bbar.shape, const),
                  pl.BlockSpec(a8.shape, const),
                  pl.BlockSpec(cmat.shape, const),
                  pl.BlockSpec(dskip.shape, const),
                  pl.BlockSpec(wglu.shape, const)],
        out_specs=pl.BlockSpec((tt, 8, w), lambda bg, i: (i, bg, 0)),
        out_shape=jax.ShapeDtypeStruct((lp, bsz, w), F32),
        scratch_shapes=[pltpu.VMEM((tt * 8, nst2), F32), pltpu.VMEM((8, nst2), F32)],
        compiler_params=_params(("parallel", "arbitrary")),
        name="s5_scan",
    )(u3, bbar, a8, cmat, dskip, wglu)


_LEVEL_HALVES = (32, 16, 8, 4, 2, 1)


def _hgrn_constants():
    c = HGRN_CHUNK
    t = np.arange(c)[:, None]
    j = np.arange(c)[None, :]
    mats = [(j <= t), (j > t)]
    for half in _LEVEL_HALVES:
        upper = (t & half) != 0
        seg_lo = t - (t % half)
        seg_hi = t | (half - 1)
        mats.append(np.where(upper, (j >= seg_lo) & (j <= t), (j > t) & (j <= seg_hi)))
    dmat = np.concatenate(mats, axis=0).astype(np.float32)
    s = np.arange(4 * c)[None, :] % c
    x = t ^ s
    hb = np.floor(np.log2(np.maximum(x, 1))).astype(np.int32)
    lv = np.where(s > t, -1, np.where(s == t, 6, hb)).astype(np.int32)
    hm = (np.arange(4 * c)[:, None] // c == np.arange(4 * c)[None, :] // c).astype(np.float32)
    return dmat, lv, hm


def _hgrn_kernel(zq_ref, zf_ref, zi_ref, zg_ref, lb_ref, ng_ref, dmat_ref, lv_ref, hm_ref,
                 o_ref, st_ref):
    c = HGRN_CHUNK
    w = zq_ref.shape[2]
    nchunks = zq_ref.shape[1] // c
    st_ref[...] = jnp.zeros_like(st_ref)

    lb = lb_ref[...]
    lbm = jnp.maximum(lb, LB_FLOOR)
    one_m_lb = 1.0 - lb
    k_off = lb - lbm
    ng = ng_ref[...]
    row = lax.broadcasted_iota(jnp.int32, (c, w), 0)

    def chunk(ci, carry):
        r0 = pl.multiple_of(ci * c, c)
        zq = zq_ref[0, pl.ds(r0, c), :]
        zf = zf_ref[0, pl.ds(r0, c), :]
        v = zi_ref[0, pl.ds(r0, c), :]
        zg = zg_ref[0, pl.ds(r0, c), :]
        hm = hm_ref[...]
        hmb = hm.astype(BF16)
        lv = lv_ref[...]

        q = zq * _sigmoid(zq)
        logf = jnp.log(lbm + one_m_lb * _sigmoid(zf))
        k = one_m_lb * _sigmoid(-zf) + k_off

        dmat = dmat_ref[...]
        hi, mid, lo = _split3(logf)
        e_all = jnp.exp(_dot(dmat, hi) + _dot(dmat, mid) + _dot(dmat, lo))
        e_cum = e_all[0:c]
        e_rest = e_all[c:2 * c]

        p = jnp.zeros((c, 4 * c), F32)
        for li, half in enumerate(_LEVEL_HALVES):
            e_l = e_all[(2 + li) * c:(3 + li) * c]
            upper = (row & half) != 0
            q_l = jnp.where(upper, q * e_l, 0.0).astype(BF16)
            k_l = jnp.where(upper, 0.0, k * e_l)
            r_l = (jnp.concatenate([k_l] * HGRN_HEADS, axis=0) * hm).astype(BF16)
            s_l = _dot_nt(q_l, r_l)
            p = jnp.where(lv == int(math.log2(half)), s_l, p)
        s_d = _dot((q * k).astype(BF16), hmb)
        p = jnp.where(lv == 6, s_d, p)

        vbd = (jnp.concatenate([v] * HGRN_HEADS, axis=0) * hm).astype(BF16)
        st = st_ref[...]
        o = _dot(p.astype(BF16), vbd) + _dot_nt((q * e_cum).astype(BF16), st.astype(BF16))

        upd = _dot_tn(v.astype(BF16), (k * e_rest).astype(BF16))
        st_ref[...] = st * e_cum[c - 1:c, :] + upd * hm

        oo_hi, oo_lo = _split2(o * o)
        ms = (_dot(oo_hi, hmb) + _dot(oo_lo, hmb)) * (1.0 / (w // HGRN_HEADS))
        out = o * lax.rsqrt(ms + RMS_EPS) * ng * (zg * _sigmoid(zg))
        o_ref[0, pl.ds(r0, c), :] = out
        return carry

    lax.fori_loop(0, nchunks, chunk, 0)


def _hgrn(zq, zf, zi, zg, lb, ng):
    bsz, lp, w = zq.shape
    dmat, lv, hm = _hgrn_constants()
    dmat = jnp.asarray(dmat, BF16)
    lv = jnp.asarray(lv)
    hm = jnp.asarray(hm, F32)
    seq = pl.BlockSpec((1, lp, w), lambda b: (b, 0, 0))
    const = lambda b: (0, 0)
    return pl.pallas_call(
        _hgrn_kernel,
        grid=(bsz,),
        in_specs=[seq, seq, seq, seq,
                  pl.BlockSpec(lb.shape, const), pl.BlockSpec(ng.shape, const),
                  pl.BlockSpec(dmat.shape, const), pl.BlockSpec(lv.shape, const),
                  pl.BlockSpec(hm.shape, const)],
        out_specs=seq,
        out_shape=jax.ShapeDtypeStruct((bsz, lp, w), F32),
        scratch_shapes=[pltpu.VMEM((w, w), F32)],
        compiler_params=_params(("parallel",)),
        name="hgrn2",
    )(zq, zf, zi, zg, lb, ng, dmat, lv, hm)


def _attn_kernel(q_ref, k1_ref, k2_ref, v_ref, lam_ref, g_ref, o_ref, *, tq, out_scale):
    i = pl.program_id(1)
    qpos = i * tq + lax.broadcasted_iota(jnp.int32, (tq, tq), 0)
    kloc = lax.broadcasted_iota(jnp.int32, (tq, tq), 1)
    lam = lam_ref[...]
    g = g_ref[...]

    for hd in range(ATTN_HEADS):
        cs = slice(hd * 128, (hd + 1) * 128)
        qh = q_ref[0, :, cs]

        def body(j, carry):
            k0 = pl.multiple_of(j * tq, tq)
            kpos = k0 + kloc
            valid = (kpos <= qpos) & (kpos >= PAD)
            vb = v_ref[0, pl.ds(k0, tq), cs]
            new = []
            for kr, (m, l, acc) in zip((k1_ref, k2_ref), carry):
                s = _dot_nt(qh, kr[0, pl.ds(k0, tq), cs])
                s = jnp.where(valid, s, MASK_VALUE)
                m_new = jnp.maximum(m, jnp.max(s, axis=-1, keepdims=True))
                alpha = jnp.exp(m - m_new)
                p = jnp.exp(s - m_new)
                l_new = alpha * l + jnp.sum(p, axis=-1, keepdims=True)
                acc_new = alpha * acc + _dot(p.astype(BF16), vb)
                new.append((m_new, l_new, acc_new))
            return tuple(new)

        init = (jnp.full((tq, 1), MASK_VALUE, F32), jnp.zeros((tq, 1), F32), jnp.zeros((tq, 128), F32))
        (_, l1, a1), (_, l2, a2) = lax.fori_loop(0, i + 1, body, (init, init))
        o = a1 / l1 - lam * (a2 / l2)
        ms = jnp.mean(o * o, axis=-1, keepdims=True)
        o_ref[0, :, cs] = (o * lax.rsqrt(ms + RMS_EPS) * g * out_scale).astype(o_ref.dtype)


def _attn(q, k1, k2, v, lam_row, g_row, out_scale):
    bsz, lp, w = q.shape
    tq = 128
    qs = pl.BlockSpec((1, tq, w), lambda b, i: (b, i, 0))
    full = pl.BlockSpec((1, lp, w), lambda b, i: (b, 0, 0))
    row = pl.BlockSpec((1, 128), lambda b, i: (0, 0))
    return pl.pallas_call(
        functools.partial(_attn_kernel, tq=tq, out_scale=out_scale),
        grid=(bsz, lp // tq),
        in_specs=[qs, full, full, full, row, row],
        out_specs=qs,
        out_shape=jax.ShapeDtypeStruct((bsz, lp, w), BF16),
        compiler_params=_params(("parallel", "arbitrary")),
        name="diff_attn",
    )(q, k1, k2, v, lam_row, g_row)


def _router_gates(logits):
    lane = lax.broadcasted_iota(jnp.int32, logits.shape, 1)
    lanef = lane.astype(F32)
    big = 1e9
    is_g = lane < MOE_GROUPS
    glog = jnp.where(is_g, logits, MASK_VALUE)
    gmax = jnp.max(glog, axis=-1, keepdims=True)
    gsum = jnp.sum(jnp.where(is_g, jnp.exp(glog - gmax), 0.0), axis=-1, keepdims=True)
    gp = 1.0 / gsum
    gi = jnp.min(jnp.where(is_g & (glog == gmax), lanef, big), axis=-1, keepdims=True)
    e_idx = lane - MOE_GROUPS
    in_group = (e_idx >= 0) & (e_idx < MOE_EXPERTS) & ((e_idx >> 2) == gi.astype(jnp.int32))
    elog = jnp.where(in_group, logits, MASK_VALUE)
    v1 = jnp.max(elog, axis=-1, keepdims=True)
    i1 = jnp.min(jnp.where(in_group & (elog == v1), lanef, big), axis=-1, keepdims=True)
    rest = in_group & (lanef != i1)
    elog2 = jnp.where(rest, logits, MASK_VALUE)
    v2 = jnp.max(elog2, axis=-1, keepdims=True)
    i2 = jnp.min(jnp.where(rest & (elog2 == v2), lanef, big), axis=-1, keepdims=True)
    e21 = jnp.exp(v2 - v1)
    den = 1.0 / (1.0 + e21)
    return jnp.where(lanef == i1, gp * den, 0.0) + jnp.where(lanef == i2, gp * e21 * den, 0.0)


def _out_proj_kernel(h_ref, ys_ref, yr_ref, ya_ref, wo_ref, gn_ref, wr_hi_ref, wr_lo_ref, br_ref,
                     hout_ref, hn_ref, gates_ref, *, tl):
    acc = _dot(ys_ref[...].astype(BF16), wo_ref[0:256, :])
    acc += _dot(yr_ref[0].astype(BF16), wo_ref[256:512, :])
    acc += _dot(ya_ref[0], wo_ref[512:1024, :])
    h = h_ref[0] + acc
    rowpos = pl.program_id(1) * tl + lax.broadcasted_iota(jnp.int32, h.shape, 0)
    h = jnp.where(rowpos >= PAD, h, 0.0)
    hout_ref[0] = h
    ms = jnp.mean(h * h, axis=-1, keepdims=True)
    hn = h * lax.rsqrt(ms + RMS_EPS) * gn_ref[...]
    hn_ref[0] = hn.astype(BF16)
    x_hi, x_lo = _split2(hn)
    logits = (_dot(x_hi, wr_hi_ref[...]) + _dot(x_hi, wr_lo_ref[...]) + _dot(x_lo, wr_hi_ref[...])
              + br_ref[...])
    gates_ref[0] = _router_gates(logits)


def _out_proj(h, ys2d, yr, ya, wo_bf, gn, wr_hi, wr_lo, br):
    bsz, lp, d = h.shape
    tl = _row_tile(lp, 544)
    row = lambda b, i: (b, i, 0)
    const = lambda b, i: (0, 0)
    return pl.pallas_call(
        functools.partial(_out_proj_kernel, tl=tl),
        grid=(bsz, lp // tl),
        in_specs=[pl.BlockSpec((1, tl, d), row),
                  pl.BlockSpec((tl, 256), lambda b, i: (i, b)),
                  pl.BlockSpec((1, tl, 256), row),
                  pl.BlockSpec((1, tl, 512), row),
                  pl.BlockSpec(wo_bf.shape, const),
                  pl.BlockSpec(gn.shape, const),
                  pl.BlockSpec(wr_hi.shape, const),
                  pl.BlockSpec(wr_lo.shape, const),
                  pl.BlockSpec(br.shape, const)],
        out_specs=(pl.BlockSpec((1, tl, d), row), pl.BlockSpec((1, tl, d), row),
                   pl.BlockSpec((1, tl, ROUTER_LANES), row)),
        out_shape=(jax.ShapeDtypeStruct((bsz, lp, d), F32),
                   jax.ShapeDtypeStruct((bsz, lp, d), BF16),
                   jax.ShapeDtypeStruct((bsz, lp, ROUTER_LANES), F32)),
        compiler_params=_params(("parallel", "parallel")),
        name="out_proj_router",
    )(h, ys2d, yr, ya, wo_bf, gn, wr_hi, wr_lo, br)


def _moe_kernel(x_ref, gates_ref, h_ref, wg_ref, wu_ref, wd_ref, o_ref, acc_ref):
    e = pl.program_id(1)

    @pl.when(e == 0)
    def _():
        acc_ref[...] = h_ref[...]

    x = x_ref[...]
    a = _dot(x, wg_ref[0])
    b = _dot(x, wu_ref[0])
    he = (a * _sigmoid(a)) * b
    gates = gates_ref[...]
    lane = lax.broadcasted_iota(jnp.int32, gates.shape, 1)
    ge = jnp.sum(jnp.where(lane == e + MOE_GROUPS, gates, 0.0), axis=-1, keepdims=True)
    acc_ref[...] += _dot((he * ge).astype(BF16), wd_ref[0])

    @pl.when(e == pl.num_programs(1) - 1)
    def _():
        o_ref[...] = acc_ref[...]


def _moe_dense(x, gates, h, wg, wu, wd):
    t, d = x.shape
    ne, _, ff = wg.shape
    tm = _row_tile(t, 1024)
    tok = lambda i, e: (i, 0)
    return pl.pallas_call(
        _moe_kernel,
        grid=(t // tm, ne),
        in_specs=[pl.BlockSpec((tm, d), tok),
                  pl.BlockSpec((tm, ROUTER_LANES), tok),
                  pl.BlockSpec((tm, d), tok),
                  pl.BlockSpec((1, d, ff), lambda i, e: (e, 0, 0)),
                  pl.BlockSpec((1, d, ff), lambda i, e: (e, 0, 0)),
                  pl.BlockSpec((1, ff, d), lambda i, e: (e, 0, 0))],
        out_specs=pl.BlockSpec((tm, d), tok),
        out_shape=jax.ShapeDtypeStruct((t, d), F32),
        scratch_shapes=[pltpu.VMEM((tm, d), F32)],
        compiler_params=_params(("parallel", "arbitrary")),
        name="moe_experts",
    )(x, gates, h, wg, wu, wd)


def _final_norm_kernel(h_ref, g_ref, o_ref):
    x = h_ref[...]
    ms = jnp.mean(x * x, axis=-1, keepdims=True)
    o_ref[...] = x * lax.rsqrt(ms + RMS_EPS) * g_ref[...]


def _final_norm(h, g, seq):
    bsz, lp, d = h.shape
    tb = 8 if bsz % 8 == 0 else 1
    return pl.pallas_call(
        _final_norm_kernel,
        grid=(bsz // tb, seq // 128),
        in_specs=[pl.BlockSpec((tb, 128, d), lambda b, i: (b, i + 1, 0)),
                  pl.BlockSpec((1, 1, d), lambda b, i: (0, 0, 0))],
        out_specs=pl.BlockSpec((tb, 128, d), lambda b, i: (b, i, 0)),
        out_shape=jax.ShapeDtypeStruct((bsz, seq, d), F32),
        compiler_params=_params(("parallel", "parallel")),
        name="final_norm",
    )(h, g.reshape(1, 1, d))


def _s5_tables(lam_re, lam_im, log_dt, b_re, b_im, c_re, c_im):
    ng, ns = lam_re.shape
    lr = lam_re.astype(F32)
    li = lam_im.astype(F32)
    dt = jnp.exp(log_dt.astype(F32))[:, None]
    mag = jnp.exp(lr * dt)
    abar_r = mag * jnp.cos(li * dt)
    abar_i = mag * jnp.sin(li * dt)
    den = lr * lr + li * li
    zr = abar_r - 1.0
    zi = abar_i
    fr = (zr * lr + zi * li) / den
    fi = (zi * lr - zr * li) / den
    br = b_re.astype(F32)
    bi = b_im.astype(F32)
    bb_r = fr[..., None] * br - fi[..., None] * bi
    bb_i = fr[..., None] * bi + fi[..., None] * br
    eye = jnp.eye(ng, dtype=F32)
    nch = b_re.shape[2]
    to_b = lambda m: jnp.einsum('gnc,gh->gchn', m, eye).reshape(ng * nch, ng * ns)
    bbar = jnp.concatenate([to_b(bb_r), to_b(bb_i)], axis=1).astype(BF16)
    to_c = lambda m: jnp.einsum('gcn,gh->gnhc', m.astype(F32), eye).reshape(ng * ns, ng * nch)
    cmat = jnp.concatenate([to_c(c_re), -to_c(c_im)], axis=0).astype(BF16)
    a_row = jnp.concatenate([abar_r.reshape(1, -1), abar_i.reshape(1, -1)], axis=1)
    a8 = jnp.broadcast_to(a_row, (8, a_row.shape[1]))
    return bbar, a8, cmat


def _rope_tables(lp):
    half = 32
    inv_freq = 1.0 / (ROPE_THETA ** (jnp.arange(0, 2 * half, 2, dtype=F32) / (2 * half)))
    pos = jnp.arange(lp, dtype=F32) - float(PAD)
    ang = pos[:, None] * inv_freq[None, :]
    ang = jnp.concatenate([ang, ang, ang, ang], axis=-1)
    cos = jnp.cos(ang)
    sin = jnp.sin(ang)
    first = (jnp.arange(128) % 64) < half
    sina = jnp.where(first[None, :], -sin, 0.0)
    sinb = jnp.where(first[None, :], 0.0, sin)
    return cos, sina, sinb


def kernel(x, meta_tokens, norm_mix_g, w_in, s5_lambda_re, s5_lambda_im, s5_log_dt, s5_b_re, s5_b_im, s5_c_re, s5_c_im, s5_d, s5_w_glu, hgrn_lower_bounds, hgrn_norm_g, diff_lambda_q1, diff_lambda_k1, diff_lambda_q2, diff_lambda_k2, diff_subln_g, w_out, norm_ffn_g, moe_w_group, moe_b_group, moe_w_expert, moe_b_expert, moe_w_gate, moe_w_up, moe_w_down, final_norm_g):
    bsz, seq, d = x.shape
    depth = w_in.shape[0]
    lp = PAD + N_META + seq
    assert lp % 128 == 0 and bsz % 8 == 0

    meta = jnp.broadcast_to(meta_tokens.astype(F32)[None], (bsz, N_META, d))
    h = jnp.concatenate([jnp.zeros((bsz, PAD, d), F32), meta, x.astype(F32)], axis=1)

    cos, sina, sinb = _rope_tables(lp)
    lb_w = jax.nn.softmax(hgrn_lower_bounds.astype(F32), axis=0)
    lower_bounds = jnp.cumsum(lb_w, axis=0) - lb_w[0:1]

    for layer in range(depth):
        lam_init = 0.8 - 0.6 * math.exp(-0.3 * layer)
        u2d, hq, hf, hi, hg, aq, ak1, ak2, av = _in_proj(
            h, norm_mix_g[layer].reshape(1, d).astype(F32), w_in[layer].astype(BF16), cos, sina, sinb)

        bbar, a8, cmat = _s5_tables(s5_lambda_re[layer], s5_lambda_im[layer], s5_log_dt[layer],
                                    s5_b_re[layer], s5_b_im[layer], s5_c_re[layer], s5_c_im[layer])
        y_ssm = _s5(u2d.reshape(lp, bsz, 256), bbar, a8, cmat,
                    s5_d[layer].reshape(1, -1).astype(F32), s5_w_glu[layer].astype(BF16))

        ng = jnp.tile(hgrn_norm_g[layer].astype(F32), HGRN_HEADS).reshape(1, -1)
        y_rec = _hgrn(hq, hf, hi, hg, lower_bounds[layer].reshape(1, -1), ng)

        lam = (jnp.exp(jnp.sum(diff_lambda_q1[layer].astype(F32) * diff_lambda_k1[layer].astype(F32)))
               - jnp.exp(jnp.sum(diff_lambda_q2[layer].astype(F32) * diff_lambda_k2[layer].astype(F32)))
               + lam_init)
        y_att = _attn(aq, ak1, ak2, av, jnp.full((1, 128), lam, F32),
                      diff_subln_g[layer].reshape(1, -1).astype(F32), 1.0 - lam_init)

        wr = jnp.concatenate([moe_w_group[layer].astype(F32), moe_w_expert[layer].astype(F32)], axis=1)
        wr = jnp.pad(wr, ((0, 0), (0, ROUTER_LANES - wr.shape[1])))
        wr_hi = wr.astype(BF16)
        wr_lo = (wr - wr_hi.astype(F32)).astype(BF16)
        br = jnp.concatenate([moe_b_group[layer].astype(F32), moe_b_expert[layer].astype(F32)])
        br = jnp.pad(br, (0, ROUTER_LANES - br.shape[0])).reshape(1, -1)
        h, hn, gates = _out_proj(h, y_ssm.reshape(lp, bsz * 256), y_rec, y_att, w_out[layer].astype(BF16),
                                 norm_ffn_g[layer].reshape(1, d).astype(F32), wr_hi, wr_lo, br)

        h = _moe_dense(hn.reshape(bsz * lp, d), gates.reshape(bsz * lp, ROUTER_LANES),
                       h.reshape(bsz * lp, d), moe_w_gate[layer].astype(BF16),
                       moe_w_up[layer].astype(BF16), moe_w_down[layer].astype(BF16)).reshape(bsz, lp, d)

    return _final_norm(h, final_norm_g.astype(F32), seq)
```

```python
import functools
import math

import numpy as np
import jax
import jax.numpy as jnp
from jax import lax
from jax.experimental import pallas as pl
from jax.experimental.pallas import tpu as pltpu

F32 = jnp.float32
BF16 = jnp.bfloat16

N_META = 16
PAD = 112
RMS_EPS = 1e-6
MASK_VALUE = -1e30
LB_FLOOR = 1e-30
ROPE_THETA = 10000.0

S5_GROUP_CH = 16
S5_STATE = 64
HGRN_HEADS = 4
HGRN_CHUNK = 64
ATTN_HEADS = 4
ATTN_QK_DIM = 64
Q_SCALE = ATTN_QK_DIM ** -0.5 * math.log2(math.e)
MOE_GROUPS = 4
MOE_EXPERTS_PER_GROUP = 4
MOE_EXPERTS = MOE_GROUPS * MOE_EXPERTS_PER_GROUP
ROUTER_LANES = 128

VMEM_LIMIT = 56 * 1024 * 1024


def _params(sem, vmem=VMEM_LIMIT):
    return pltpu.CompilerParams(dimension_semantics=sem, vmem_limit_bytes=vmem)


def _row_tile(n, cap, mult=16):
    best = None
    for t in range(mult, min(n, cap) + 1, mult):
        if n % t == 0:
            best = t
    assert best is not None, (n, cap)
    return best


def _sigmoid(x):
    return 1.0 / (1.0 + jnp.exp(-x))


def _dot(a, b):
    return jnp.dot(a, b, preferred_element_type=F32)


def _dot_nt(a, b):
    return lax.dot_general(a, b, (((1,), (1,)), ((), ())), preferred_element_type=F32)


def _dot_tn(a, b):
    return lax.dot_general(a, b, (((0,), (0,)), ((), ())), preferred_element_type=F32)


def _split2(x):
    hi = x.astype(BF16)
    lo = (x - hi.astype(F32)).astype(BF16)
    return hi, lo


def _split3(x):
    hi = x.astype(BF16)
    r = x - hi.astype(F32)
    mid = r.astype(BF16)
    lo = (r - mid.astype(F32)).astype(BF16)
    return hi, mid, lo


def _in_proj_kernel(h_ref, g_ref, w_ref, cos_ref, sina_ref, sinb_ref,
                    u_ref, hq_ref, hf_ref, hi_ref, hg_ref, q_ref, k1_ref, k2_ref, v_ref):
    x = h_ref[0]
    ms = jnp.mean(x * x, axis=-1, keepdims=True)
    xn = (x * lax.rsqrt(ms + RMS_EPS) * g_ref[...]).astype(BF16)

    def sec(lo, hi):
        return _dot(xn, w_ref[:, lo:hi])

    u_ref[...] = sec(0, 256)
    hq_ref[0] = sec(256, 512)
    hf_ref[0] = sec(512, 768)
    hi_ref[0] = sec(768, 1024)
    hg_ref[0] = sec(1024, 1280)

    cos = cos_ref[...]
    sina = sina_ref[...]
    sinb = sinb_ref[...]
    lane = lax.broadcasted_iota(jnp.int32, cos.shape, 1)
    first = lane < 64

    def rope(t):
        return t * cos + pltpu.roll(t, 96, 1) * sina + pltpu.roll(t, 32, 1) * sinb

    for hd in range(ATTN_HEADS):
        c0 = 1280 + hd * 128
        q = rope(sec(c0, c0 + 128)) * Q_SCALE
        q_ref[0, :, hd * 128:(hd + 1) * 128] = q.astype(BF16)
        c0 = 1792 + hd * 128
        k = rope(sec(c0, c0 + 128))
        k1_ref[0, :, hd * 128:(hd + 1) * 128] = jnp.where(first, k, 0.0).astype(BF16)
        k2_ref[0, :, hd * 128:(hd + 1) * 128] = jnp.where(first, 0.0, k).astype(BF16)
    v_ref[0] = sec(2304, 2816).astype(BF16)


def _in_proj(h, g, w_bf, cos, sina, sinb):
    bsz, lp, d = h.shape
    tl = _row_tile(lp, 544)
    nl = lp // tl
    row = lambda b, i: (b, i, 0)
    tab = pl.BlockSpec((tl, 128), lambda b, i: (i, 0))
    out_shape = (
        jax.ShapeDtypeStruct((lp, bsz * 256), F32),
        jax.ShapeDtypeStruct((bsz, lp, 256), F32),
        jax.ShapeDtypeStruct((bsz, lp, 256), F32),
        jax.ShapeDtypeStruct((bsz, lp, 256), F32),
        jax.ShapeDtypeStruct((bsz, lp, 256), F32),
        jax.ShapeDtypeStruct((bsz, lp, 512), BF16),
        jax.ShapeDtypeStruct((bsz, lp, 512), BF16),
        jax.ShapeDtypeStruct((bsz, lp, 512), BF16),
        jax.ShapeDtypeStruct((bsz, lp, 512), BF16),
    )
    s256 = pl.BlockSpec((1, tl, 256), row)
    s512 = pl.BlockSpec((1, tl, 512), row)
    return pl.pallas_call(
        _in_proj_kernel,
        grid=(bsz, nl),
        in_specs=[pl.BlockSpec((1, tl, d), row),
                  pl.BlockSpec((1, d), lambda b, i: (0, 0)),
                  pl.BlockSpec(w_bf.shape, lambda b, i: (0, 0)),
                  tab, tab, tab],
        out_specs=(pl.BlockSpec((tl, 256), lambda b, i: (i, b)), s256, s256, s256, s256,
                   s512, s512, s512, s512),
        out_shape=out_shape,
        compiler_params=_params(("parallel", "parallel")),
        name="in_proj",
    )(h, g, w_bf, cos, sina, sinb)


def _s5_kernel(u_ref, bbar_ref, a_ref, cmat_ref, d_ref, wglu_ref, y_ref, xs_ref, st_ref):
    tt = u_ref.shape[0]
    nst = a_ref.shape[1] // 2

    @pl.when(pl.program_id(1) == 0)
    def _():
        st_ref[...] = jnp.zeros_like(st_ref)

    u2 = u_ref[...].reshape(tt * 8, u_ref.shape[2])
    xs_ref[...] = _dot(u2.astype(BF16), bbar_ref[...])

    ar = a_ref[:, :nst]
    ai = a_ref[:, nst:]

    def step(t, carry):
        sr, si = carry
        r0 = pl.multiple_of(t * 8, 8)
        xr = xs_ref[pl.ds(r0, 8), :nst]
        xi = xs_ref[pl.ds(r0, 8), nst:]
        nr = ar * sr - ai * si + xr
        ni = ar * si + ai * sr + xi
        xs_ref[pl.ds(r0, 8), :nst] = nr
        xs_ref[pl.ds(r0, 8), nst:] = ni
        return nr, ni

    sr, si = lax.fori_loop(0, tt, step, (st_ref[:, :nst], st_ref[:, nst:]))
    st_ref[:, :nst] = sr
    st_ref[:, nst:] = si

    y = _dot(xs_ref[...].astype(BF16), cmat_ref[...]) + d_ref[...] * u2
    g = 0.5 * y * (1.0 + jnp.tanh(0.7978845608028654 * (y + 0.044715 * (y * y * y))))
    gl = _dot(g.astype(BF16), wglu_ref[...])
    w = gl.shape[1] // 2
    out = gl[:, :w] * _sigmoid(gl[:, w:])
    y_ref[...] = out.reshape(y_ref.shape)


def _s5(u3, bbar, a8, cmat, dskip, wglu):
    lp, bsz, w = u3.shape
    tt = 128
    nst2 = a8.shape[1]
    const = lambda bg, i: (0, 0)
    return pl.pallas_call(
        _s5_kernel,
        grid=(bsz // 8, lp // tt),
        in_specs=[pl.BlockSpec((tt, 8, w), lambda bg, i: (i, bg, 0)),
                  pl.BlockSpec(bbar.shape, const),
                  pl.BlockSpec(a8.shape, const),
                  pl.BlockSpec(cmat.shape, const),
                  pl.BlockSpec(dskip.shape, const),
                  pl.BlockSpec(wglu.shape, const)],
        out_specs=pl.BlockSpec((tt, 8, w), lambda bg, i: (i, bg, 0)),
        out_shape=jax.ShapeDtypeStruct((lp, bsz, w), F32),
        scratch_shapes=[pltpu.VMEM((tt * 8, nst2), F32), pltpu.VMEM((8, nst2), F32)],
        compiler_params=_params(("parallel", "arbitrary")),
        name="s5_scan",
    )(u3, bbar, a8, cmat, dskip, wglu)


_LEVEL_HALVES = (32, 16, 8, 4, 2, 1)


def _hgrn_constants():
    c = HGRN_CHUNK
    t = np.arange(c)[:, None]
    j = np.arange(c)[None, :]
    mats = [(j <= t), (j > t)]
    for half in _LEVEL_HALVES:
        upper = (t & half) != 0
        seg_lo = t - (t % half)
        seg_hi = t | (half - 1)
        mats.append(np.where(upper, (j >= seg_lo) & (j <= t), (j > t) & (j <= seg_hi)))
    dmat = np.concatenate(mats, axis=0).astype(np.float32)
    s = np.arange(4 * c)[None, :] % c
    x = t ^ s
    hb = np.floor(np.log2(np.maximum(x, 1))).astype(np.int32)
    lv = np.where(s > t, -1, np.where(s == t, 6, hb)).astype(np.int32)
    hm = (np.arange(4 * c)[:, None] // c == np.arange(4 * c)[None, :] // c).astype(np.float32)
    return dmat, lv, hm


def _hgrn_kernel(zq_ref, zf_ref, zi_ref, zg_ref, lb_ref, ng_ref, dmat_ref, lv_ref, hm_ref,
                 o_ref, st_ref):
    c = HGRN_CHUNK
    w = zq_ref.shape[2]
    nchunks = zq_ref.shape[1] // c
    st_ref[...] = jnp.zeros_like(st_ref)

    lb = lb_ref[...]
    lbm = jnp.maximum(lb, LB_FLOOR)
    one_m_lb = 1.0 - lb
    k_off = lb - lbm
    ng = ng_ref[...]
    row = lax.broadcasted_iota(jnp.int32, (c, w), 0)

    def chunk(ci, carry):
        r0 = pl.multiple_of(ci * c, c)
        zq = zq_ref[0, pl.ds(r0, c), :]
        zf = zf_ref[0, pl.ds(r0, c), :]
        v = zi_ref[0, pl.ds(r0, c), :]
        zg = zg_ref[0, pl.ds(r0, c), :]
        hm = hm_ref[...]
        hmb = hm.astype(BF16)
        lv = lv_ref[...]

        q = zq * _sigmoid(zq)
        logf = jnp.log(lbm + one_m_lb * _sigmoid(zf))
        k = one_m_lb * _sigmoid(-zf) + k_off

        dmat = dmat_ref[...]
        hi, mid, lo = _split3(logf)
        e_all = jnp.exp(_dot(dmat, hi) + _dot(dmat, mid) + _dot(dmat, lo))
        e_cum = e_all[0:c]
        e_rest = e_all[c:2 * c]

        p = jnp.zeros((c, 4 * c), F32)
        for li, half in enumerate(_LEVEL_HALVES):
            e_l = e_all[(2 + li) * c:(3 + li) * c]
            upper = (row & half) != 0
            q_l = jnp.where(upper, q * e_l, 0.0).astype(BF16)
            k_l = jnp.where(upper, 0.0, k * e_l)
            r_l = (jnp.concatenate([k_l] * HGRN_HEADS, axis=0) * hm).astype(BF16)
            s_l = _dot_nt(q_l, r_l)
            p = jnp.where(lv == int(math.log2(half)), s_l, p)
        s_d = _dot((q * k).astype(BF16), hmb)
        p = jnp.where(lv == 6, s_d, p)

        vbd = (jnp.concatenate([v] * HGRN_HEADS, axis=0) * hm).astype(BF16)
        st = st_ref[...]
        o = _dot(p.astype(BF16), vbd) + _dot_nt((q * e_cum).astype(BF16), st.astype(BF16))

        upd = _dot_tn(v.astype(BF16), (k * e_rest).astype(BF16))
        st_ref[...] = st * e_cum[c - 1:c, :] + upd * hm

        oo_hi, oo_lo = _split2(o * o)
        ms = (_dot(oo_hi, hmb) + _dot(oo_lo, hmb)) * (1.0 / (w // HGRN_HEADS))
        out = o * lax.rsqrt(ms + RMS_EPS) * ng * (zg * _sigmoid(zg))
        o_ref[0, pl.ds(r0, c), :] = out
        return carry

    lax.fori_loop(0, nchunks, chunk, 0)


def _hgrn(zq, zf, zi, zg, lb, ng):
    bsz, lp, w = zq.shape
    dmat, lv, hm = _hgrn_constants()
    dmat = jnp.asarray(dmat, BF16)
    lv = jnp.asarray(lv)
    hm = jnp.asarray(hm, F32)
    seq = pl.BlockSpec((1, lp, w), lambda b: (b, 0, 0))
    const = lambda b: (0, 0)
    return pl.pallas_call(
        _hgrn_kernel,
        grid=(bsz,),
        in_specs=[seq, seq, seq, seq,
                  pl.BlockSpec(lb.shape, const), pl.BlockSpec(ng.shape, const),
                  pl.BlockSpec(dmat.shape, const), pl.BlockSpec(lv.shape, const),
                  pl.BlockSpec(hm.shape, const)],
        out_specs=seq,
        out_shape=jax.ShapeDtypeStruct((bsz, lp, w), F32),
        scratch_shapes=[pltpu.VMEM((w, w), F32)],
        compiler_params=_params(("parallel",)),
        name="hgrn2",
    )(zq, zf, zi, zg, lb, ng, dmat, lv, hm)


def _attn_kernel(q_ref, k1_ref, k2_ref, v_ref, lam_ref, g_ref, o_ref, m_ref, l_ref, acc_ref,
                 *, tq, tk, out_scale):
    i = pl.program_id(1)
    lp = k1_ref.shape[1]
    q0 = i * tq
    nkv = (q0 + tq + tk - 1) // tk
    n_full = jnp.maximum(q0 // tk, 1)
    kloc = lax.broadcasted_iota(jnp.int32, (tq, tk), 1)
    k_minus_q = kloc - lax.broadcasted_iota(jnp.int32, (tq, tk), 0)
    nrep = tk // 128
    lam = lam_ref[...]
    g = g_ref[...]

    for hd in range(ATTN_HEADS):
        cs = slice(hd * 128, (hd + 1) * 128)
        qh = q_ref[0, :, cs]
        m_ref[...] = jnp.full(m_ref.shape, MASK_VALUE, F32)
        l_ref[...] = jnp.zeros(l_ref.shape, F32)
        acc_ref[...] = jnp.zeros(acc_ref.shape, F32)

        def block(j, masked):
            k0 = pl.multiple_of(jnp.minimum(j * tk, lp - tk), 128)
            vb = v_ref[0, pl.ds(k0, tk), cs]
            if masked:
                valid = (k_minus_q <= q0 - k0) & (kloc >= jnp.maximum(PAD, j * tk) - k0)
            for c, kr in enumerate((k1_ref, k2_ref)):
                s = _dot_nt(qh, kr[0, pl.ds(k0, tk), cs])
                if masked:
                    s = jnp.where(valid, s, MASK_VALUE)
                m_prev = m_ref[c]
                m_next = jnp.maximum(m_prev, jnp.max(s, axis=-1, keepdims=True))
                alpha = jnp.exp2(m_prev - m_next)
                p = jnp.exp2(s - jnp.concatenate([m_next] * nrep, axis=1))
                psum = p[:, 0:128]
                for r in range(1, nrep):
                    psum = psum + p[:, r * 128:(r + 1) * 128]
                m_ref[c] = m_next
                l_ref[c] = alpha * l_ref[c] + psum
                acc_ref[c] = alpha * acc_ref[c] + _dot(p.astype(BF16), vb)

        def run(masked):
            def body(j, carry):
                block(j, masked)
                return carry
            return body

        block(0, True)
        lax.fori_loop(1, n_full, run(False), 0)
        lax.fori_loop(n_full, nkv, run(True), 0)

        l1 = jnp.sum(l_ref[0], axis=-1, keepdims=True)
        l2 = jnp.sum(l_ref[1], axis=-1, keepdims=True)
        o = acc_ref[0] / l1 - lam * (acc_ref[1] / l2)
        ms = jnp.mean(o * o, axis=-1, keepdims=True)
        o_ref[0, :, cs] = (o * lax.rsqrt(ms + RMS_EPS) * g * out_scale).astype(o_ref.dtype)


def _attn(q, k1, k2, v, lam_row, g_row, out_scale):
    bsz, lp, w = q.shape
    tq = _row_tile(lp, 544)
    tk = 256
    qs = pl.BlockSpec((1, tq, w), lambda b, i: (b, i, 0))
    full = pl.BlockSpec((1, lp, w), lambda b, i: (b, 0, 0))
    row = pl.BlockSpec((1, 128), lambda b, i: (0, 0))
    return pl.pallas_call(
        functools.partial(_attn_kernel, tq=tq, tk=tk, out_scale=out_scale),
        grid=(bsz, lp // tq),
        in_specs=[qs, full, full, full, row, row],
        out_specs=qs,
        out_shape=jax.ShapeDtypeStruct((bsz, lp, w), BF16),
        scratch_shapes=[pltpu.VMEM((2, tq, 128), F32), pltpu.VMEM((2, tq, 128), F32),
                        pltpu.VMEM((2, tq, 128), F32)],
        compiler_params=_params(("parallel", "arbitrary")),
        name="diff_attn",
    )(q, k1, k2, v, lam_row, g_row)


def _router_gates(logits):
    lane = lax.broadcasted_iota(jnp.int32, logits.shape, 1)
    lanef = lane.astype(F32)
    big = 1e9
    is_g = lane < MOE_GROUPS
    glog = jnp.where(is_g, logits, MASK_VALUE)
    gmax = jnp.max(glog, axis=-1, keepdims=True)
    gsum = jnp.sum(jnp.where(is_g, jnp.exp(glog - gmax), 0.0), axis=-1, keepdims=True)
    gp = 1.0 / gsum
    gi = jnp.min(jnp.where(is_g & (glog == gmax), lanef, big), axis=-1, keepdims=True)
    e_idx = lane - MOE_GROUPS
    in_group = (e_idx >= 0) & (e_idx < MOE_EXPERTS) & ((e_idx >> 2) == gi.astype(jnp.int32))
    elog = jnp.where(in_group, logits, MASK_VALUE)
    v1 = jnp.max(elog, axis=-1, keepdims=True)
    i1 = jnp.min(jnp.where(in_group & (elog == v1), lanef, big), axis=-1, keepdims=True)
    rest = in_group & (lanef != i1)
    elog2 = jnp.where(rest, logits, MASK_VALUE)
    v2 = jnp.max(elog2, axis=-1, keepdims=True)
    i2 = jnp.min(jnp.where(rest & (elog2 == v2), lanef, big), axis=-1, keepdims=True)
    e21 = jnp.exp(v2 - v1)
    den = 1.0 / (1.0 + e21)
    return jnp.where(lanef == i1, gp * den, 0.0) + jnp.where(lanef == i2, gp * e21 * den, 0.0)


def _out_proj_kernel(h_ref, ys_ref, yr_ref, ya_ref, wo_ref, gn_ref, wr_hi_ref, wr_lo_ref, br_ref,
                     hout_ref, hn_ref, gates_ref, *, tl):
    acc = _dot(ys_ref[...].astype(BF16), wo_ref[0:256, :])
    acc += _dot(yr_ref[0].astype(BF16), wo_ref[256:512, :])
    acc += _dot(ya_ref[0], wo_ref[512:1024, :])
    h = h_ref[0] + acc
    rowpos = pl.program_id(1) * tl + lax.broadcasted_iota(jnp.int32, h.shape, 0)
    h = jnp.where(rowpos >= PAD, h, 0.0)
    hout_ref[0] = h
    ms = jnp.mean(h * h, axis=-1, keepdims=True)
    hn = h * lax.rsqrt(ms + RMS_EPS) * gn_ref[...]
    hn_ref[0] = hn.astype(BF16)
    x_hi, x_lo = _split2(hn)
    logits = (_dot(x_hi, wr_hi_ref[...]) + _dot(x_hi, wr_lo_ref[...]) + _dot(x_lo, wr_hi_ref[...])
              + br_ref[...])
    gates_ref[0] = _router_gates(logits)


def _out_proj(h, ys2d, yr, ya, wo_bf, gn, wr_hi, wr_lo, br):
    bsz, lp, d = h.shape
    tl = _row_tile(lp, 544)
    row = lambda b, i: (b, i, 0)
    const = lambda b, i: (0, 0)
    return pl.pallas_call(
        functools.partial(_out_proj_kernel, tl=tl),
        grid=(bsz, lp // tl),
        in_specs=[pl.BlockSpec((1, tl, d), row),
                  pl.BlockSpec((tl, 256), lambda b, i: (i, b)),
                  pl.BlockSpec((1, tl, 256), row),
                  pl.BlockSpec((1, tl, 512), row),
                  pl.BlockSpec(wo_bf.shape, const),
                  pl.BlockSpec(gn.shape, const),
                  pl.BlockSpec(wr_hi.shape, const),
                  pl.BlockSpec(wr_lo.shape, const),
                  pl.BlockSpec(br.shape, const)],
        out_specs=(pl.BlockSpec((1, tl, d), row), pl.BlockSpec((1, tl, d), row),
                   pl.BlockSpec((1, tl, ROUTER_LANES), row)),
        out_shape=(jax.ShapeDtypeStruct((bsz, lp, d), F32),
                   jax.ShapeDtypeStruct((bsz, lp, d), BF16),
                   jax.ShapeDtypeStruct((bsz, lp, ROUTER_LANES), F32)),
        compiler_params=_params(("parallel", "parallel")),
        name="out_proj_router",
    )(h, ys2d, yr, ya, wo_bf, gn, wr_hi, wr_lo, br)


def _moe_kernel(x_ref, gates_ref, h_ref, wg_ref, wu_ref, wd_ref, o_ref, acc_ref):
    e = pl.program_id(1)

    @pl.when(e == 0)
    def _():
        acc_ref[...] = h_ref[...]

    x = x_ref[...]
    a = _dot(x, wg_ref[0])
    b = _dot(x, wu_ref[0])
    he = (a * _sigmoid(a)) * b
    gates = gates_ref[...]
    lane = lax.broadcasted_iota(jnp.int32, gates.shape, 1)
    ge = jnp.sum(jnp.where(lane == e + MOE_GROUPS, gates, 0.0), axis=-1, keepdims=True)
    acc_ref[...] += _dot((he * ge).astype(BF16), wd_ref[0])

    @pl.when(e == pl.num_programs(1) - 1)
    def _():
        o_ref[...] = acc_ref[...]


def _moe_dense(x, gates, h, wg, wu, wd):
    t, d = x.shape
    ne, _, ff = wg.shape
    tm = _row_tile(t, 1024)
    tok = lambda i, e: (i, 0)
    return pl.pallas_call(
        _moe_kernel,
        grid=(t // tm, ne),
        in_specs=[pl.BlockSpec((tm, d), tok),
                  pl.BlockSpec((tm, ROUTER_LANES), tok),
                  pl.BlockSpec((tm, d), tok),
                  pl.BlockSpec((1, d, ff), lambda i, e: (e, 0, 0)),
                  pl.BlockSpec((1, d, ff), lambda i, e: (e, 0, 0)),
                  pl.BlockSpec((1, ff, d), lambda i, e: (e, 0, 0))],
        out_specs=pl.BlockSpec((tm, d), tok),
        out_shape=jax.ShapeDtypeStruct((t, d), F32),
        scratch_shapes=[pltpu.VMEM((tm, d), F32)],
        compiler_params=_params(("parallel", "arbitrary")),
        name="moe_experts",
    )(x, gates, h, wg, wu, wd)


def _final_norm_kernel(h_ref, g_ref, o_ref):
    x = h_ref[...]
    ms = jnp.mean(x * x, axis=-1, keepdims=True)
    o_ref[...] = x * lax.rsqrt(ms + RMS_EPS) * g_ref[...]


def _final_norm(h, g, seq):
    bsz, lp, d = h.shape
    tb = 8 if bsz % 8 == 0 else 1
    return pl.pallas_call(
        _final_norm_kernel,
        grid=(bsz // tb, seq // 128),
        in_specs=[pl.BlockSpec((tb, 128, d), lambda b, i: (b, i + 1, 0)),
                  pl.BlockSpec((1, 1, d), lambda b, i: (0, 0, 0))],
        out_specs=pl.BlockSpec((tb, 128, d), lambda b, i: (b, i, 0)),
        out_shape=jax.ShapeDtypeStruct((bsz, seq, d), F32),
        compiler_params=_params(("parallel", "parallel")),
        name="final_norm",
    )(h, g.reshape(1, 1, d))


def _s5_tables(lam_re, lam_im, log_dt, b_re, b_im, c_re, c_im):
    ng, ns = lam_re.shape
    lr = lam_re.astype(F32)
    li = lam_im.astype(F32)
    dt = jnp.exp(log_dt.astype(F32))[:, None]
    mag = jnp.exp(lr * dt)
    abar_r = mag * jnp.cos(li * dt)
    abar_i = mag * jnp.sin(li * dt)
    den = lr * lr + li * li
    zr = abar_r - 1.0
    zi = abar_i
    fr = (zr * lr + zi * li) / den
    fi = (zi * lr - zr * li) / den
    br = b_re.astype(F32)
    bi = b_im.astype(F32)
    bb_r = fr[..., None] * br - fi[..., None] * bi
    bb_i = fr[..., None] * bi + fi[..., None] * br
    eye = jnp.eye(ng, dtype=F32)
    nch = b_re.shape[2]
    to_b = lambda m: jnp.einsum('gnc,gh->gchn', m, eye).reshape(ng * nch, ng * ns)
    bbar = jnp.concatenate([to_b(bb_r), to_b(bb_i)], axis=1).astype(BF16)
    to_c = lambda m: jnp.einsum('gcn,gh->gnhc', m.astype(F32), eye).reshape(ng * ns, ng * nch)
    cmat = jnp.concatenate([to_c(c_re), -to_c(c_im)], axis=0).astype(BF16)
    a_row = jnp.concatenate([abar_r.reshape(1, -1), abar_i.reshape(1, -1)], axis=1)
    a8 = jnp.broadcast_to(a_row, (8, a_row.shape[1]))
    return bbar, a8, cmat


def _rope_tables(lp):
    half = 32
    inv_freq = 1.0 / (ROPE_THETA ** (jnp.arange(0, 2 * half, 2, dtype=F32) / (2 * half)))
    pos = jnp.arange(lp, dtype=F32) - float(PAD)
    ang = pos[:, None] * inv_freq[None, :]
    ang = jnp.concatenate([ang, ang, ang, ang], axis=-1)
    cos = jnp.cos(ang)
    sin = jnp.sin(ang)
    first = (jnp.arange(128) % 64) < half
    sina = jnp.where(first[None, :], -sin, 0.0)
    sinb = jnp.where(first[None, :], 0.0, sin)
    return cos, sina, sinb


def kernel(x, meta_tokens, norm_mix_g, w_in, s5_lambda_re, s5_lambda_im, s5_log_dt, s5_b_re, s5_b_im, s5_c_re, s5_c_im, s5_d, s5_w_glu, hgrn_lower_bounds, hgrn_norm_g, diff_lambda_q1, diff_lambda_k1, diff_lambda_q2, diff_lambda_k2, diff_subln_g, w_out, norm_ffn_g, moe_w_group, moe_b_group, moe_w_expert, moe_b_expert, moe_w_gate, moe_w_up, moe_w_down, final_norm_g):
    bsz, seq, d = x.shape
    depth = w_in.shape[0]
    lp = PAD + N_META + seq
    assert lp % 128 == 0 and bsz % 8 == 0

    meta = jnp.broadcast_to(meta_tokens.astype(F32)[None], (bsz, N_META, d))
    h = jnp.concatenate([jnp.zeros((bsz, PAD, d), F32), meta, x.astype(F32)], axis=1)

    cos, sina, sinb = _rope_tables(lp)
    lb_w = jax.nn.softmax(hgrn_lower_bounds.astype(F32), axis=0)
    lower_bounds = jnp.cumsum(lb_w, axis=0) - lb_w[0:1]

    for layer in range(depth):
        lam_init = 0.8 - 0.6 * math.exp(-0.3 * layer)
        u2d, hq, hf, hi, hg, aq, ak1, ak2, av = _in_proj(
            h, norm_mix_g[layer].reshape(1, d).astype(F32), w_in[layer].astype(BF16), cos, sina, sinb)

        bbar, a8, cmat = _s5_tables(s5_lambda_re[layer], s5_lambda_im[layer], s5_log_dt[layer],
                                    s5_b_re[layer], s5_b_im[layer], s5_c_re[layer], s5_c_im[layer])
        y_ssm = _s5(u2d.reshape(lp, bsz, 256), bbar, a8, cmat,
                    s5_d[layer].reshape(1, -1).astype(F32), s5_w_glu[layer].astype(BF16))

        ng = jnp.tile(hgrn_norm_g[layer].astype(F32), HGRN_HEADS).reshape(1, -1)
        y_rec = _hgrn(hq, hf, hi, hg, lower_bounds[layer].reshape(1, -1), ng)

        lam = (jnp.exp(jnp.sum(diff_lambda_q1[layer].astype(F32) * diff_lambda_k1[layer].astype(F32)))
               - jnp.exp(jnp.sum(diff_lambda_q2[layer].astype(F32) * diff_lambda_k2[layer].astype(F32)))
               + lam_init)
        y_att = _attn(aq, ak1, ak2, av, jnp.full((1, 128), lam, F32),
                      diff_subln_g[layer].reshape(1, -1).astype(F32), 1.0 - lam_init)

        wr = jnp.concatenate([moe_w_group[layer].astype(F32), moe_w_expert[layer].astype(F32)], axis=1)
        wr = jnp.pad(wr, ((0, 0), (0, ROUTER_LANES - wr.shape[1])))
        wr_hi = wr.astype(BF16)
        wr_lo = (wr - wr_hi.astype(F32)).astype(BF16)
        br = jnp.concatenate([moe_b_group[layer].astype(F32), moe_b_expert[layer].astype(F32)])
        br = jnp.pad(br, (0, ROUTER_LANES - br.shape[0])).reshape(1, -1)
        h, hn, gates = _out_proj(h, y_ssm.reshape(lp, bsz * 256), y_rec, y_att, w_out[layer].astype(BF16),
                                 norm_ffn_g[layer].reshape(1, d).astype(F32), wr_hi, wr_lo, br)

        h = _moe_dense(hn.reshape(bsz * lp, d), gates.reshape(bsz * lp, ROUTER_LANES),
                       h.reshape(bsz * lp, d), moe_w_gate[layer].astype(BF16),
                       moe_w_up[layer].astype(BF16), moe_w_down[layer].astype(BF16)).reshape(bsz, lp, d)

    return _final_norm(h, final_norm_g.astype(F32), seq)
```

```python
import functools
import math

import numpy as np
import jax
import jax.numpy as jnp
from jax import lax
from jax.experimental import pallas as pl
from jax.experimental.pallas import tpu as pltpu

F32 = jnp.float32
BF16 = jnp.bfloat16

N_META = 16
PAD = 112
RMS_EPS = 1e-6
MASK_VALUE = -1e30
LB_FLOOR = 1e-30
ROPE_THETA = 10000.0

S5_GROUP_CH = 16
S5_STATE = 64
HGRN_HEADS = 4
HGRN_CHUNK = 64
ATTN_HEADS = 4
ATTN_QK_DIM = 64
Q_SCALE = ATTN_QK_DIM ** -0.5 * math.log2(math.e)
MOE_GROUPS = 4
MOE_EXPERTS_PER_GROUP = 4
MOE_EXPERTS = MOE_GROUPS * MOE_EXPERTS_PER_GROUP
ROUTER_LANES = 128

VMEM_LIMIT = 56 * 1024 * 1024


def _params(sem, vmem=VMEM_LIMIT):
    return pltpu.CompilerParams(dimension_semantics=sem, vmem_limit_bytes=vmem)


def _row_tile(n, cap, mult=16):
    best = None
    for t in range(mult, min(n, cap) + 1, mult):
        if n % t == 0:
            best = t
    assert best is not None, (n, cap)
    return best


def _sigmoid(x):
    return 1.0 / (1.0 + jnp.exp(-x))


def _dot(a, b):
    return jnp.dot(a, b, preferred_element_type=F32)


def _dot_nt(a, b):
    return lax.dot_general(a, b, (((1,), (1,)), ((), ())), preferred_element_type=F32)


def _dot_tn(a, b):
    return lax.dot_general(a, b, (((0,), (0,)), ((), ())), preferred_element_type=F32)


SLAB = 8


def _load_tokens(ref, n, first=0):
    return jnp.concatenate([ref[pl.ds(first * SLAB + s, n, stride=SLAB), :] for s in range(SLAB)], axis=1)


def _store_tokens(ref, val):
    n = val.shape[0]
    for s in range(SLAB):
        ref[pl.ds(s, n, stride=SLAB), :] = val[:, s * 128:(s + 1) * 128]


def _split2(x):
    hi = x.astype(BF16)
    lo = (x - hi.astype(F32)).astype(BF16)
    return hi, lo


def _split3(x):
    hi = x.astype(BF16)
    r = x - hi.astype(F32)
    mid = r.astype(BF16)
    lo = (r - mid.astype(F32)).astype(BF16)
    return hi, mid, lo


def _in_proj_kernel(h_ref, g_ref, w_ref, cos_ref, sina_ref, sinb_ref,
                    u_ref, hq_ref, hf_ref, hi_ref, hg_ref, q_ref, k1_ref, k2_ref, v_ref):
    x = _load_tokens(h_ref, u_ref.shape[0])
    ms = jnp.mean(x * x, axis=-1, keepdims=True)
    xn = (x * lax.rsqrt(ms + RMS_EPS) * g_ref[...]).astype(BF16)

    def sec(lo, hi):
        return _dot(xn, w_ref[:, lo:hi])

    u_ref[...] = sec(0, 256)
    hq_ref[0] = sec(256, 512)
    hf_ref[0] = sec(512, 768)
    hi_ref[0] = sec(768, 1024)
    hg_ref[0] = sec(1024, 1280)

    cos = cos_ref[...]
    sina = sina_ref[...]
    sinb = sinb_ref[...]
    lane = lax.broadcasted_iota(jnp.int32, cos.shape, 1)
    first = lane < 64

    def rope(t):
        return t * cos + pltpu.roll(t, 96, 1) * sina + pltpu.roll(t, 32, 1) * sinb

    for hd in range(ATTN_HEADS):
        c0 = 1280 + hd * 128
        q = rope(sec(c0, c0 + 128)) * Q_SCALE
        q_ref[0, :, hd * 128:(hd + 1) * 128] = q.astype(BF16)
        c0 = 1792 + hd * 128
        k = rope(sec(c0, c0 + 128))
        k1_ref[0, :, hd * 128:(hd + 1) * 128] = jnp.where(first, k, 0.0).astype(BF16)
        k2_ref[0, :, hd * 128:(hd + 1) * 128] = jnp.where(first, 0.0, k).astype(BF16)
    v_ref[0] = sec(2304, 2816).astype(BF16)


def _in_proj(h_slab, bsz, lp, g, w_bf, cos, sina, sinb):
    d = SLAB * 128
    tl = _row_tile(lp, 544)
    nl = lp // tl
    row = lambda b, i: (b, i, 0)
    tab = pl.BlockSpec((tl, 128), lambda b, i: (i, 0))
    out_shape = (
        jax.ShapeDtypeStruct((lp, bsz * 256), F32),
        jax.ShapeDtypeStruct((bsz, lp, 256), F32),
        jax.ShapeDtypeStruct((bsz, lp, 256), F32),
        jax.ShapeDtypeStruct((bsz, lp, 256), F32),
        jax.ShapeDtypeStruct((bsz, lp, 256), F32),
        jax.ShapeDtypeStruct((bsz, lp, 512), BF16),
        jax.ShapeDtypeStruct((bsz, lp, 512), BF16),
        jax.ShapeDtypeStruct((bsz, lp, 512), BF16),
        jax.ShapeDtypeStruct((bsz, lp, 512), BF16),
    )
    s256 = pl.BlockSpec((1, tl, 256), row)
    s512 = pl.BlockSpec((1, tl, 512), row)
    return pl.pallas_call(
        _in_proj_kernel,
        grid=(bsz, nl),
        in_specs=[pl.BlockSpec((tl * SLAB, 128), lambda b, i: (b * nl + i, 0)),
                  pl.BlockSpec((1, d), lambda b, i: (0, 0)),
                  pl.BlockSpec(w_bf.shape, lambda b, i: (0, 0)),
                  tab, tab, tab],
        out_specs=(pl.BlockSpec((tl, 256), lambda b, i: (i, b)), s256, s256, s256, s256,
                   s512, s512, s512, s512),
        out_shape=out_shape,
        compiler_params=_params(("parallel", "parallel")),
        name="in_proj",
    )(h_slab, g, w_bf, cos, sina, sinb)


def _s5_kernel(u_ref, bbar_ref, a_ref, cmat_ref, d_ref, wglu_ref, y_ref, xs_ref, st_ref):
    tt = u_ref.shape[0]
    nst = a_ref.shape[1] // 2

    @pl.when(pl.program_id(1) == 0)
    def _():
        st_ref[...] = jnp.zeros_like(st_ref)

    u2 = u_ref[...].reshape(tt * 8, u_ref.shape[2])
    xs_ref[...] = _dot(u2.astype(BF16), bbar_ref[...])

    ar = a_ref[:, :nst]
    ai = a_ref[:, nst:]

    def step(t, carry):
        sr, si = carry
        r0 = pl.multiple_of(t * 8, 8)
        xr = xs_ref[pl.ds(r0, 8), :nst]
        xi = xs_ref[pl.ds(r0, 8), nst:]
        nr = ar * sr - ai * si + xr
        ni = ar * si + ai * sr + xi
        xs_ref[pl.ds(r0, 8), :nst] = nr
        xs_ref[pl.ds(r0, 8), nst:] = ni
        return nr, ni

    sr, si = lax.fori_loop(0, tt, step, (st_ref[:, :nst], st_ref[:, nst:]))
    st_ref[:, :nst] = sr
    st_ref[:, nst:] = si

    y = _dot(xs_ref[...].astype(BF16), cmat_ref[...]) + d_ref[...] * u2
    g = 0.5 * y * (1.0 + jnp.tanh(0.7978845608028654 * (y + 0.044715 * (y * y * y))))
    gl = _dot(g.astype(BF16), wglu_ref[...])
    w = gl.shape[1] // 2
    out = gl[:, :w] * _sigmoid(gl[:, w:])
    y_ref[...] = out.reshape(y_ref.shape)


def _s5(u3, bbar, a8, cmat, dskip, wglu):
    lp, bsz, w = u3.shape
    tt = 128
    nst2 = a8.shape[1]
    const = lambda bg, i: (0, 0)
    return pl.pallas_call(
        _s5_kernel,
        grid=(bsz // 8, lp // tt),
        in_specs=[pl.BlockSpec((tt, 8, w), lambda bg, i: (i, bg, 0)),
                  pl.BlockSpec(bbar.shape, const),
                  pl.BlockSpec(a8.shape, const),
                  pl.BlockSpec(cmat.shape, const),
                  pl.BlockSpec(dskip.shape, const),
                  pl.BlockSpec(wglu.shape, const)],
        out_specs=pl.BlockSpec((tt, 8, w), lambda bg, i: (i, bg, 0)),
        out_shape=jax.ShapeDtypeStruct((lp, bsz, w), F32),
        scratch_shapes=[pltpu.VMEM((tt * 8, nst2), F32), pltpu.VMEM((8, nst2), F32)],
        compiler_params=_params(("parallel", "arbitrary")),
        name="s5_scan",
    )(u3, bbar, a8, cmat, dskip, wglu)


_LEVEL_HALVES = (32, 16, 8, 4, 2, 1)


def _hgrn_constants():
    c = HGRN_CHUNK
    t = np.arange(c)[:, None]
    j = np.arange(c)[None, :]
    mats = [(j <= t), (j > t)]
    for half in _LEVEL_HALVES:
        upper = (t & half) != 0
        seg_lo = t - (t % half)
        seg_hi = t | (half - 1)
        mats.append(np.where(upper, (j >= seg_lo) & (j <= t), (j > t) & (j <= seg_hi)))
    dmat = np.concatenate(mats, axis=0).astype(np.float32)
    s = np.arange(4 * c)[None, :] % c
    x = t ^ s
    hb = np.floor(np.log2(np.maximum(x, 1))).astype(np.int32)
    lv = np.where(s > t, -1, np.where(s == t, 6, hb)).astype(np.int32)
    hm = (np.arange(4 * c)[:, None] // c == np.arange(4 * c)[None, :] // c).astype(np.float32)
    return dmat, lv, hm


def _hgrn_kernel(zq_ref, zf_ref, zi_ref, zg_ref, lb_ref, ng_ref, dmat_ref, lv_ref, hm_ref,
                 o_ref, st_ref):
    c = HGRN_CHUNK
    w = zq_ref.shape[2]
    nchunks = zq_ref.shape[1] // c
    st_ref[...] = jnp.zeros_like(st_ref)

    lb = lb_ref[...]
    lbm = jnp.maximum(lb, LB_FLOOR)
    one_m_lb = 1.0 - lb
    k_off = lb - lbm
    ng = ng_ref[...]
    row = lax.broadcasted_iota(jnp.int32, (c, w), 0)

    def chunk(ci, carry):
        r0 = pl.multiple_of(ci * c, c)
        zq = zq_ref[0, pl.ds(r0, c), :]
        zf = zf_ref[0, pl.ds(r0, c), :]
        v = zi_ref[0, pl.ds(r0, c), :]
        zg = zg_ref[0, pl.ds(r0, c), :]
        hm = hm_ref[...]
        hmb = hm.astype(BF16)
        lv = lv_ref[...]

        q = zq * _sigmoid(zq)
        logf = jnp.log(lbm + one_m_lb * _sigmoid(zf))
        k = one_m_lb * _sigmoid(-zf) + k_off

        dmat = dmat_ref[...]
        hi, mid, lo = _split3(logf)
        e_all = jnp.exp(_dot(dmat, hi) + _dot(dmat, mid) + _dot(dmat, lo))
        e_cum = e_all[0:c]
        e_rest = e_all[c:2 * c]

        p = jnp.zeros((c, 4 * c), F32)
        for li, half in enumerate(_LEVEL_HALVES):
            e_l = e_all[(2 + li) * c:(3 + li) * c]
            upper = (row & half) != 0
            q_l = jnp.where(upper, q * e_l, 0.0).astype(BF16)
            k_l = jnp.where(upper, 0.0, k * e_l)
            r_l = (jnp.concatenate([k_l] * HGRN_HEADS, axis=0) * hm).astype(BF16)
            s_l = _dot_nt(q_l, r_l)
            p = jnp.where(lv == int(math.log2(half)), s_l, p)
        s_d = _dot((q * k).astype(BF16), hmb)
        p = jnp.where(lv == 6, s_d, p)

        vbd = (jnp.concatenate([v] * HGRN_HEADS, axis=0) * hm).astype(BF16)
        st = st_ref[...]
        o = _dot(p.astype(BF16), vbd) + _dot_nt((q * e_cum).astype(BF16), st.astype(BF16))

        upd = _dot_tn(v.astype(BF16), (k * e_rest).astype(BF16))
        st_ref[...] = st * e_cum[c - 1:c, :] + upd * hm

        oo_hi, oo_lo = _split2(o * o)
        ms = (_dot(oo_hi, hmb) + _dot(oo_lo, hmb)) * (1.0 / (w // HGRN_HEADS))
        out = o * lax.rsqrt(ms + RMS_EPS) * ng * (zg * _sigmoid(zg))
        o_ref[0, pl.ds(r0, c), :] = out
        return carry

    lax.fori_loop(0, nchunks, chunk, 0)


def _hgrn(zq, zf, zi, zg, lb, ng):
    bsz, lp, w = zq.shape
    dmat, lv, hm = _hgrn_constants()
    dmat = jnp.asarray(dmat, BF16)
    lv = jnp.asarray(lv)
    hm = jnp.asarray(hm, F32)
    seq = pl.BlockSpec((1, lp, w), lambda b: (b, 0, 0))
    const = lambda b: (0, 0)
    return pl.pallas_call(
        _hgrn_kernel,
        grid=(bsz,),
        in_specs=[seq, seq, seq, seq,
                  pl.BlockSpec(lb.shape, const), pl.BlockSpec(ng.shape, const),
                  pl.BlockSpec(dmat.shape, const), pl.BlockSpec(lv.shape, const),
                  pl.BlockSpec(hm.shape, const)],
        out_specs=seq,
        out_shape=jax.ShapeDtypeStruct((bsz, lp, w), F32),
        scratch_shapes=[pltpu.VMEM((w, w), F32)],
        compiler_params=_params(("parallel",)),
        name="hgrn2",
    )(zq, zf, zi, zg, lb, ng, dmat, lv, hm)


def _attn_kernel(q_ref, k1_ref, k2_ref, v_ref, lam_ref, g_ref, o_ref, m_ref, l_ref, acc_ref,
                 *, tq, tk, out_scale):
    i = pl.program_id(1)
    lp = k1_ref.shape[1]
    q0 = i * tq
    nkv = (q0 + tq + tk - 1) // tk
    n_full = jnp.maximum(q0 // tk, 1)
    kloc = lax.broadcasted_iota(jnp.int32, (tq, tk), 1)
    k_minus_q = kloc - lax.broadcasted_iota(jnp.int32, (tq, tk), 0)
    nrep = tk // 128
    lam = lam_ref[...]
    g = g_ref[...]

    for hd in range(ATTN_HEADS):
        cs = slice(hd * 128, (hd + 1) * 128)
        qh = q_ref[0, :, cs]
        m_ref[...] = jnp.full(m_ref.shape, MASK_VALUE, F32)
        l_ref[...] = jnp.zeros(l_ref.shape, F32)
        acc_ref[...] = jnp.zeros(acc_ref.shape, F32)

        def block(j, masked):
            k0 = pl.multiple_of(jnp.minimum(j * tk, lp - tk), 128)
            vb = v_ref[0, pl.ds(k0, tk), cs]
            if masked:
                valid = (k_minus_q <= q0 - k0) & (kloc >= jnp.maximum(PAD, j * tk) - k0)
            for c, kr in enumerate((k1_ref, k2_ref)):
                s = _dot_nt(qh, kr[0, pl.ds(k0, tk), cs])
                if masked:
                    s = jnp.where(valid, s, MASK_VALUE)
                m_prev = m_ref[c]
                m_next = jnp.maximum(m_prev, jnp.max(s, axis=-1, keepdims=True))
                alpha = jnp.exp2(m_prev - m_next)
                p = jnp.exp2(s - jnp.concatenate([m_next] * nrep, axis=1))
                psum = p[:, 0:128]
                for r in range(1, nrep):
                    psum = psum + p[:, r * 128:(r + 1) * 128]
                m_ref[c] = m_next
                l_ref[c] = alpha * l_ref[c] + psum
                acc_ref[c] = alpha * acc_ref[c] + _dot(p.astype(BF16), vb)

        def run(masked):
            def body(j, carry):
                block(j, masked)
                return carry
            return body

        block(0, True)
        lax.fori_loop(1, n_full, run(False), 0)
        lax.fori_loop(n_full, nkv, run(True), 0)

        l1 = jnp.sum(l_ref[0], axis=-1, keepdims=True)
        l2 = jnp.sum(l_ref[1], axis=-1, keepdims=True)
        o = acc_ref[0] / l1 - lam * (acc_ref[1] / l2)
        ms = jnp.mean(o * o, axis=-1, keepdims=True)
        o_ref[0, :, cs] = (o * lax.rsqrt(ms + RMS_EPS) * g * out_scale).astype(o_ref.dtype)


def _attn(q, k1, k2, v, lam_row, g_row, out_scale):
    bsz, lp, w = q.shape
    tq = _row_tile(lp, 544)
    tk = 256
    qs = pl.BlockSpec((1, tq, w), lambda b, i: (b, i, 0))
    full = pl.BlockSpec((1, lp, w), lambda b, i: (b, 0, 0))
    row = pl.BlockSpec((1, 128), lambda b, i: (0, 0))
    return pl.pallas_call(
        functools.partial(_attn_kernel, tq=tq, tk=tk, out_scale=out_scale),
        grid=(bsz, lp // tq),
        in_specs=[qs, full, full, full, row, row],
        out_specs=qs,
        out_shape=jax.ShapeDtypeStruct((bsz, lp, w), BF16),
        scratch_shapes=[pltpu.VMEM((2, tq, 128), F32), pltpu.VMEM((2, tq, 128), F32),
                        pltpu.VMEM((2, tq, 128), F32)],
        compiler_params=_params(("parallel", "arbitrary")),
        name="diff_attn",
    )(q, k1, k2, v, lam_row, g_row)


def _router_logits(hn, wr_hi_ref, wr_lo_ref, br_ref):
    x_hi, x_lo = _split2(hn)
    return (_dot(x_hi, wr_hi_ref[...]) + _dot(x_hi, wr_lo_ref[...]) + _dot(x_lo, wr_hi_ref[...])
            + br_ref[...])


def _router_group(logits):
    lane = lax.broadcasted_iota(jnp.int32, logits.shape, 1)
    is_g = lane < MOE_GROUPS
    glog = jnp.where(is_g, logits, MASK_VALUE)
    gmax = jnp.max(glog, axis=-1, keepdims=True)
    gsum = jnp.sum(jnp.where(is_g, jnp.exp(glog - gmax), 0.0), axis=-1, keepdims=True)
    gi = jnp.min(jnp.where(is_g & (glog == gmax), lane.astype(F32), 1e9), axis=-1, keepdims=True)
    return gi, 1.0 / gsum


def _router_gates(logits, gi, gp):
    lane = lax.broadcasted_iota(jnp.int32, logits.shape, 1)
    lanef = lane.astype(F32)
    big = 1e9
    e_idx = lane - MOE_GROUPS
    in_group = (e_idx >= 0) & (e_idx < MOE_EXPERTS) & ((e_idx >> 2) == gi)
    elog = jnp.where(in_group, logits, MASK_VALUE)
    v1 = jnp.max(elog, axis=-1, keepdims=True)
    i1 = jnp.min(jnp.where(in_group & (elog == v1), lanef, big), axis=-1, keepdims=True)
    rest = in_group & (lanef != i1)
    elog2 = jnp.where(rest, logits, MASK_VALUE)
    v2 = jnp.max(elog2, axis=-1, keepdims=True)
    i2 = jnp.min(jnp.where(rest & (elog2 == v2), lanef, big), axis=-1, keepdims=True)
    e21 = jnp.exp(v2 - v1)
    den = 1.0 / (1.0 + e21)
    return jnp.where(lanef == i1, gp * den, 0.0) + jnp.where(lanef == i2, gp * e21 * den, 0.0)


def _out_proj_kernel(h_ref, ys_ref, yr_ref, ya_ref, wo_ref, gn_ref, wr_hi_ref, wr_lo_ref, br_ref,
                     hout_ref, gi_ref, *, tl):
    acc = _dot(ys_ref[...].astype(BF16), wo_ref[0:256, :])
    acc += _dot(yr_ref[0].astype(BF16), wo_ref[256:512, :])
    acc += _dot(ya_ref[0], wo_ref[512:1024, :])
    h = _load_tokens(h_ref, tl) + acc
    rowpos = pl.program_id(1) * tl + lax.broadcasted_iota(jnp.int32, h.shape, 0)
    h = jnp.where(rowpos >= PAD, h, 0.0)
    _store_tokens(hout_ref, h)
    ms = jnp.mean(h * h, axis=-1, keepdims=True)
    hn = h * lax.rsqrt(ms + RMS_EPS) * gn_ref[...]
    gi, _ = _router_group(_router_logits(hn, wr_hi_ref, wr_lo_ref, br_ref))
    gi_ref[0] = gi.astype(jnp.int32)


def _out_proj(h_slab, bsz, lp, ys2d, yr, ya, wo_bf, gn, wr_hi, wr_lo, br):
    d = SLAB * 128
    tl = _row_tile(lp, 544)
    nl = lp // tl
    row = lambda b, i: (b, i, 0)
    const = lambda b, i: (0, 0)
    slab = pl.BlockSpec((tl * SLAB, 128), lambda b, i: (b * nl + i, 0))
    return pl.pallas_call(
        functools.partial(_out_proj_kernel, tl=tl),
        grid=(bsz, nl),
        in_specs=[slab,
                  pl.BlockSpec((tl, 256), lambda b, i: (i, b)),
                  pl.BlockSpec((1, tl, 256), row),
                  pl.BlockSpec((1, tl, 512), row),
                  pl.BlockSpec(wo_bf.shape, const),
                  pl.BlockSpec(gn.shape, const),
                  pl.BlockSpec(wr_hi.shape, const),
                  pl.BlockSpec(wr_lo.shape, const),
                  pl.BlockSpec(br.shape, const)],
        out_specs=(slab, pl.BlockSpec((1, tl, 1), row)),
        out_shape=(jax.ShapeDtypeStruct((bsz * lp * SLAB, 128), F32),
                   jax.ShapeDtypeStruct((bsz, lp, 1), jnp.int32)),
        compiler_params=_params(("parallel", "parallel")),
        name="out_proj_router",
    )(h_slab, ys2d, yr, ya, wo_bf, gn, wr_hi, wr_lo, br)


def _moe_kernel(grp_ref, nval_ref, src_ref, h_hbm, gn_ref, wr_hi_ref, wr_lo_ref, br_ref,
                wg_ref, wu_ref, wd_ref, out_hbm, xbuf, ybuf, sem, *, tm):
    p = pl.program_id(0)
    n = nval_ref[p]
    g = grp_ref[p]

    @pl.when(p == 0)
    def _():
        xbuf[...] = jnp.zeros_like(xbuf)

    def row_copy(r, gather):
        t = src_ref[0, 0, r]
        rows = pl.ds(pl.multiple_of(r * SLAB, SLAB), SLAB)
        if gather:
            return pltpu.make_async_copy(h_hbm.at[t], xbuf.at[rows, :], sem.at[0])
        return pltpu.make_async_copy(ybuf.at[rows, :], out_hbm.at[t], sem.at[1])

    def for_rows(fn):
        def body(r, carry):
            fn(r)
            return carry
        lax.fori_loop(0, n, body, 0)

    @pl.when(n > 0)
    def _():
        for_rows(lambda r: row_copy(r, True).start())
        for_rows(lambda r: row_copy(r, True).wait())

        h = _load_tokens(xbuf, tm)
        ms = jnp.mean(h * h, axis=-1, keepdims=True)
        hn = h * lax.rsqrt(ms + RMS_EPS) * gn_ref[...]
        logits = _router_logits(hn, wr_hi_ref, wr_lo_ref, br_ref)
        lane = lax.broadcasted_iota(jnp.int32, logits.shape, 1)
        is_g = lane < MOE_GROUPS
        glog = jnp.where(is_g, logits, MASK_VALUE)
        gmax = jnp.max(glog, axis=-1, keepdims=True)
        gsum = jnp.sum(jnp.where(is_g, jnp.exp(glog - gmax), 0.0), axis=-1, keepdims=True)
        lg = jnp.sum(jnp.where(lane == g, logits, 0.0), axis=-1, keepdims=True)
        gates = _router_gates(logits, g, jnp.exp(lg - gmax) / gsum)

        x = hn.astype(BF16)
        y = h
        for e in range(MOE_EXPERTS_PER_GROUP):
            ge = jnp.sum(jnp.where(lane == MOE_GROUPS + MOE_EXPERTS_PER_GROUP * g + e, gates, 0.0),
                         axis=-1, keepdims=True)
            a = _dot(x, wg_ref[e])
            b = _dot(x, wu_ref[e])
            he = (a * _sigmoid(a)) * b * ge
            y = y + _dot(he.astype(BF16), wd_ref[e])
        _store_tokens(ybuf, y)

        for_rows(lambda r: row_copy(r, False).start())
        for_rows(lambda r: row_copy(r, False).wait())


def _moe_routed(h_slab, gi, gn, wr_hi, wr_lo, br, wg, wu, wd, tm=512):
    t = gi.shape[0]
    ne, d, ff = wg.shape
    tm = _row_tile(t, tm)
    nt = t // tm + MOE_GROUPS

    order = jnp.argsort(gi, stable=True).astype(jnp.int32)
    counts = jnp.sum((gi[:, None] == jnp.arange(MOE_GROUPS, dtype=jnp.int32)[None, :]).astype(jnp.int32),
                     axis=0)
    starts = jnp.cumsum(counts) - counts
    tiles_g = (counts + tm - 1) // tm
    tile_end = jnp.cumsum(tiles_g)
    tile_start = tile_end - tiles_g
    pidx = jnp.arange(nt, dtype=jnp.int32)
    grp = jnp.minimum(jnp.sum((pidx[:, None] >= tile_end[None, :]).astype(jnp.int32), axis=1),
                      MOE_GROUPS - 1)
    j = pidx - tile_start[grp]
    nval = jnp.where(pidx < tile_end[-1], jnp.clip(counts[grp] - j * tm, 0, tm), 0).astype(jnp.int32)
    rows = starts[grp][:, None] + j[:, None] * tm + jnp.arange(tm, dtype=jnp.int32)[None, :]
    src = order[jnp.clip(rows, 0, t - 1)].reshape(nt, 1, tm)

    const2 = lambda p, grp, nval: (0, 0)
    wspec = lambda shape: pl.BlockSpec((MOE_EXPERTS_PER_GROUP,) + shape, lambda p, grp, nval: (grp[p], 0, 0))
    grid_spec = pltpu.PrefetchScalarGridSpec(
        num_scalar_prefetch=2,
        grid=(nt,),
        in_specs=[pl.BlockSpec((1, 1, tm), lambda p, grp, nval: (p, 0, 0), memory_space=pltpu.SMEM),
                  pl.BlockSpec(memory_space=pl.ANY),
                  pl.BlockSpec(gn.shape, const2),
                  pl.BlockSpec(wr_hi.shape, const2),
                  pl.BlockSpec(wr_lo.shape, const2),
                  pl.BlockSpec(br.shape, const2),
                  wspec((d, ff)), wspec((d, ff)), wspec((ff, d))],
        out_specs=pl.BlockSpec(memory_space=pl.ANY),
        scratch_shapes=[pltpu.VMEM((tm * SLAB, 128), F32), pltpu.VMEM((tm * SLAB, 128), F32),
                        pltpu.SemaphoreType.DMA((2,))],
    )
    out = pl.pallas_call(
        functools.partial(_moe_kernel, tm=tm),
        grid_spec=grid_spec,
        out_shape=jax.ShapeDtypeStruct((t, SLAB, 128), F32),
        compiler_params=_params(("arbitrary",)),
        name="moe_experts",
    )(grp, nval, src, h_slab.reshape(t, SLAB, 128), gn, wr_hi, wr_lo, br, wg, wu, wd)
    return out.reshape(t * SLAB, 128)


def _final_norm_kernel(h_ref, g_ref, o_ref, *, skip, chunk):
    g = g_ref[...]
    for c in range(o_ref.shape[1] // chunk):
        x = _load_tokens(h_ref, chunk, first=skip + c * chunk)
        ms = jnp.mean(x * x, axis=-1, keepdims=True)
        o_ref[0, c * chunk:(c + 1) * chunk, :] = x * lax.rsqrt(ms + RMS_EPS) * g


def _final_norm(h_slab, bsz, lp, g, seq):
    d = SLAB * 128
    chunk = _row_tile(seq, 512)
    return pl.pallas_call(
        functools.partial(_final_norm_kernel, skip=lp - seq, chunk=chunk),
        grid=(bsz,),
        in_specs=[pl.BlockSpec((lp * SLAB, 128), lambda b: (b, 0)),
                  pl.BlockSpec((1, d), lambda b: (0, 0))],
        out_specs=pl.BlockSpec((1, seq, d), lambda b: (b, 0, 0)),
        out_shape=jax.ShapeDtypeStruct((bsz, seq, d), F32),
        compiler_params=_params(("parallel",)),
        name="final_norm",
    )(h_slab, g.reshape(1, d))


def _s5_tables(lam_re, lam_im, log_dt, b_re, b_im, c_re, c_im):
    ng, ns = lam_re.shape
    lr = lam_re.astype(F32)
    li = lam_im.astype(F32)
    dt = jnp.exp(log_dt.astype(F32))[:, None]
    mag = jnp.exp(lr * dt)
    abar_r = mag * jnp.cos(li * dt)
    abar_i = mag * jnp.sin(li * dt)
    den = lr * lr + li * li
    zr = abar_r - 1.0
    zi = abar_i
    fr = (zr * lr + zi * li) / den
    fi = (zi * lr - zr * li) / den
    br = b_re.astype(F32)
    bi = b_im.astype(F32)
    bb_r = fr[..., None] * br - fi[..., None] * bi
    bb_i = fr[..., None] * bi + fi[..., None] * br
    eye = jnp.eye(ng, dtype=F32)
    nch = b_re.shape[2]
    to_b = lambda m: jnp.einsum('gnc,gh->gchn', m, eye).reshape(ng * nch, ng * ns)
    bbar = jnp.concatenate([to_b(bb_r), to_b(bb_i)], axis=1).astype(BF16)
    to_c = lambda m: jnp.einsum('gcn,gh->gnhc', m.astype(F32), eye).reshape(ng * ns, ng * nch)
    cmat = jnp.concatenate([to_c(c_re), -to_c(c_im)], axis=0).astype(BF16)
    a_row = jnp.concatenate([abar_r.reshape(1, -1), abar_i.reshape(1, -1)], axis=1)
    a8 = jnp.broadcast_to(a_row, (8, a_row.shape[1]))
    return bbar, a8, cmat


def _rope_tables(lp):
    half = 32
    inv_freq = 1.0 / (ROPE_THETA ** (jnp.arange(0, 2 * half, 2, dtype=F32) / (2 * half)))
    pos = jnp.arange(lp, dtype=F32) - float(PAD)
    ang = pos[:, None] * inv_freq[None, :]
    ang = jnp.concatenate([ang, ang, ang, ang], axis=-1)
    cos = jnp.cos(ang)
    sin = jnp.sin(ang)
    first = (jnp.arange(128) % 64) < half
    sina = jnp.where(first[None, :], -sin, 0.0)
    sinb = jnp.where(first[None, :], 0.0, sin)
    return cos, sina, sinb


def kernel(x, meta_tokens, norm_mix_g, w_in, s5_lambda_re, s5_lambda_im, s5_log_dt, s5_b_re, s5_b_im, s5_c_re, s5_c_im, s5_d, s5_w_glu, hgrn_lower_bounds, hgrn_norm_g, diff_lambda_q1, diff_lambda_k1, diff_lambda_q2, diff_lambda_k2, diff_subln_g, w_out, norm_ffn_g, moe_w_group, moe_b_group, moe_w_expert, moe_b_expert, moe_w_gate, moe_w_up, moe_w_down, final_norm_g):
    bsz, seq, d = x.shape
    depth = w_in.shape[0]
    lp = PAD + N_META + seq
    assert lp % 128 == 0 and bsz % 8 == 0

    meta = jnp.broadcast_to(meta_tokens.astype(F32)[None], (bsz, N_META, d))
    h = jnp.concatenate([jnp.zeros((bsz, PAD, d), F32), meta, x.astype(F32)], axis=1)
    assert d == SLAB * 128
    h = h.reshape(bsz * lp * SLAB, 128)

    cos, sina, sinb = _rope_tables(lp)
    lb_w = jax.nn.softmax(hgrn_lower_bounds.astype(F32), axis=0)
    lower_bounds = jnp.cumsum(lb_w, axis=0) - lb_w[0:1]

    for layer in range(depth):
        lam_init = 0.8 - 0.6 * math.exp(-0.3 * layer)
        u2d, hq, hf, hi, hg, aq, ak1, ak2, av = _in_proj(
            h, bsz, lp, norm_mix_g[layer].reshape(1, d).astype(F32), w_in[layer].astype(BF16), cos, sina, sinb)

        bbar, a8, cmat = _s5_tables(s5_lambda_re[layer], s5_lambda_im[layer], s5_log_dt[layer],
                                    s5_b_re[layer], s5_b_im[layer], s5_c_re[layer], s5_c_im[layer])
        y_ssm = _s5(u2d.reshape(lp, bsz, 256), bbar, a8, cmat,
                    s5_d[layer].reshape(1, -1).astype(F32), s5_w_glu[layer].astype(BF16))

        ng = jnp.tile(hgrn_norm_g[layer].astype(F32), HGRN_HEADS).reshape(1, -1)
        y_rec = _hgrn(hq, hf, hi, hg, lower_bounds[layer].reshape(1, -1), ng)

        lam = (jnp.exp(jnp.sum(diff_lambda_q1[layer].astype(F32) * diff_lambda_k1[layer].astype(F32)))
               - jnp.exp(jnp.sum(diff_lambda_q2[layer].astype(F32) * diff_lambda_k2[layer].astype(F32)))
               + lam_init)
        y_att = _attn(aq, ak1, ak2, av, jnp.full((1, 128), lam, F32),
                      diff_subln_g[layer].reshape(1, -1).astype(F32), 1.0 - lam_init)

        wr = jnp.concatenate([moe_w_group[layer].astype(F32), moe_w_expert[layer].astype(F32)], axis=1)
        wr = jnp.pad(wr, ((0, 0), (0, ROUTER_LANES - wr.shape[1])))
        wr_hi = wr.astype(BF16)
        wr_lo = (wr - wr_hi.astype(F32)).astype(BF16)
        br = jnp.concatenate([moe_b_group[layer].astype(F32), moe_b_expert[layer].astype(F32)])
        br = jnp.pad(br, (0, ROUTER_LANES - br.shape[0])).reshape(1, -1)
        gn = norm_ffn_g[layer].reshape(1, d).astype(F32)
        h, gi = _out_proj(h, bsz, lp, y_ssm.reshape(lp, bsz * 256), y_rec, y_att,
                          w_out[layer].astype(BF16), gn, wr_hi, wr_lo, br)
        h = _moe_routed(h, gi.reshape(bsz * lp), gn, wr_hi, wr_lo, br, moe_w_gate[layer].astype(BF16),
                        moe_w_up[layer].astype(BF16), moe_w_down[layer].astype(BF16))

    return _final_norm(h, bsz, lp, final_norm_g.astype(F32), seq)
```

```python
import functools
import math

import numpy as np
import jax
import jax.numpy as jnp
from jax import lax
from jax.experimental import pallas as pl
from jax.experimental.pallas import tpu as pltpu

F32 = jnp.float32
BF16 = jnp.bfloat16

N_META = 16
PAD = 112
RMS_EPS = 1e-6
MASK_VALUE = -1e30
LB_FLOOR = 1e-30
ROPE_THETA = 10000.0

S5_GROUP_CH = 16
S5_STATE = 64
HGRN_HEADS = 4
HGRN_CHUNK = 64
ATTN_HEADS = 4
ATTN_QK_DIM = 64
Q_SCALE = ATTN_QK_DIM ** -0.5 * math.log2(math.e)
MOE_GROUPS = 4
MOE_EXPERTS_PER_GROUP = 4
MOE_EXPERTS = MOE_GROUPS * MOE_EXPERTS_PER_GROUP
ROUTER_LANES = 128

VMEM_LIMIT = 56 * 1024 * 1024


def _params(sem, vmem=VMEM_LIMIT):
    return pltpu.CompilerParams(dimension_semantics=sem, vmem_limit_bytes=vmem)


def _row_tile(n, cap, mult=16):
    best = None
    for t in range(mult, min(n, cap) + 1, mult):
        if n % t == 0:
            best = t
    assert best is not None, (n, cap)
    return best


def _sigmoid(x):
    return 1.0 / (1.0 + jnp.exp(-x))


def _dot(a, b):
    return jnp.dot(a, b, preferred_element_type=F32)


def _dot_nt(a, b):
    return lax.dot_general(a, b, (((1,), (1,)), ((), ())), preferred_element_type=F32)


def _dot_tn(a, b):
    return lax.dot_general(a, b, (((0,), (0,)), ((), ())), preferred_element_type=F32)


SLAB = 8


def _load_tokens(ref, n, first=0):
    return jnp.concatenate([ref[pl.ds(first * SLAB + s, n, stride=SLAB), :] for s in range(SLAB)], axis=1)


def _store_tokens(ref, val):
    n = val.shape[0]
    for s in range(SLAB):
        ref[pl.ds(s, n, stride=SLAB), :] = val[:, s * 128:(s + 1) * 128]


def _split2(x):
    hi = x.astype(BF16)
    lo = (x - hi.astype(F32)).astype(BF16)
    return hi, lo


def _split3(x):
    hi = x.astype(BF16)
    r = x - hi.astype(F32)
    mid = r.astype(BF16)
    lo = (r - mid.astype(F32)).astype(BF16)
    return hi, mid, lo


def _in_proj_kernel(h_ref, g_ref, w_ref, cos_ref, sina_ref, sinb_ref,
                    u_ref, hq_ref, hf_ref, hi_ref, hg_ref, q_ref, k1_ref, k2_ref, v_ref):
    x = _load_tokens(h_ref, u_ref.shape[0])
    ms = jnp.mean(x * x, axis=-1, keepdims=True)
    xn = (x * lax.rsqrt(ms + RMS_EPS) * g_ref[...]).astype(BF16)

    def sec(lo, hi):
        return _dot(xn, w_ref[:, lo:hi])

    u_ref[...] = sec(0, 256)
    hq_ref[0] = sec(256, 512)
    hf_ref[0] = sec(512, 768)
    hi_ref[0] = sec(768, 1024)
    hg_ref[0] = sec(1024, 1280)

    cos = cos_ref[...]
    sina = sina_ref[...]
    sinb = sinb_ref[...]
    lane = lax.broadcasted_iota(jnp.int32, cos.shape, 1)
    first = lane < 64

    def rope(t):
        return t * cos + pltpu.roll(t, 96, 1) * sina + pltpu.roll(t, 32, 1) * sinb

    for hd in range(ATTN_HEADS):
        c0 = 1280 + hd * 128
        q = rope(sec(c0, c0 + 128)) * Q_SCALE
        q_ref[0, :, hd * 128:(hd + 1) * 128] = q.astype(BF16)
        c0 = 1792 + hd * 128
        k = rope(sec(c0, c0 + 128))
        k1_ref[0, :, hd * 128:(hd + 1) * 128] = jnp.where(first, k, 0.0).astype(BF16)
        k2_ref[0, :, hd * 128:(hd + 1) * 128] = jnp.where(first, 0.0, k).astype(BF16)
    v_ref[0] = sec(2304, 2816).astype(BF16)


def _in_proj(h_slab, bsz, lp, g, w_bf, cos, sina, sinb):
    d = SLAB * 128
    tl = _row_tile(lp, 544)
    nl = lp // tl
    row = lambda b, i: (b, i, 0)
    tab = pl.BlockSpec((tl, 128), lambda b, i: (i, 0))
    out_shape = (
        jax.ShapeDtypeStruct((lp, bsz * 256), F32),
        jax.ShapeDtypeStruct((bsz, lp, 256), F32),
        jax.ShapeDtypeStruct((bsz, lp, 256), F32),
        jax.ShapeDtypeStruct((bsz, lp, 256), F32),
        jax.ShapeDtypeStruct((bsz, lp, 256), F32),
        jax.ShapeDtypeStruct((bsz, lp, 512), BF16),
        jax.ShapeDtypeStruct((bsz, lp, 512), BF16),
        jax.ShapeDtypeStruct((bsz, lp, 512), BF16),
        jax.ShapeDtypeStruct((bsz, lp, 512), BF16),
    )
    s256 = pl.BlockSpec((1, tl, 256), row)
    s512 = pl.BlockSpec((1, tl, 512), row)
    return pl.pallas_call(
        _in_proj_kernel,
        grid=(bsz, nl),
        in_specs=[pl.BlockSpec((tl * SLAB, 128), lambda b, i: (b * nl + i, 0)),
                  pl.BlockSpec((1, d), lambda b, i: (0, 0)),
                  pl.BlockSpec(w_bf.shape, lambda b, i: (0, 0)),
                  tab, tab, tab],
        out_specs=(pl.BlockSpec((tl, 256), lambda b, i: (i, b)), s256, s256, s256, s256,
                   s512, s512, s512, s512),
        out_shape=out_shape,
        compiler_params=_params(("parallel", "parallel")),
        name="in_proj",
    )(h_slab, g, w_bf, cos, sina, sinb)


def _s5_kernel(u_ref, bbar_ref, a_ref, cmat_ref, d_ref, wglu_ref, y_ref, xs_ref, st_ref):
    tt = u_ref.shape[0]
    nst = a_ref.shape[1] // 2

    @pl.when(pl.program_id(1) == 0)
    def _():
        st_ref[...] = jnp.zeros_like(st_ref)

    u2 = u_ref[...].reshape(tt * 8, u_ref.shape[2])
    xs_ref[...] = _dot(u2.astype(BF16), bbar_ref[...])

    ar = a_ref[:, :nst]
    ai = a_ref[:, nst:]

    def step(t, carry):
        sr, si = carry
        r0 = pl.multiple_of(t * 8, 8)
        xr = xs_ref[pl.ds(r0, 8), :nst]
        xi = xs_ref[pl.ds(r0, 8), nst:]
        nr = ar * sr - ai * si + xr
        ni = ar * si + ai * sr + xi
        xs_ref[pl.ds(r0, 8), :nst] = nr
        xs_ref[pl.ds(r0, 8), nst:] = ni
        return nr, ni

    sr, si = lax.fori_loop(0, tt, step, (st_ref[:, :nst], st_ref[:, nst:]))
    st_ref[:, :nst] = sr
    st_ref[:, nst:] = si

    y = _dot(xs_ref[...].astype(BF16), cmat_ref[...]) + d_ref[...] * u2
    g = 0.5 * y * (1.0 + jnp.tanh(0.7978845608028654 * (y + 0.044715 * (y * y * y))))
    gl = _dot(g.astype(BF16), wglu_ref[...])
    w = gl.shape[1] // 2
    out = gl[:, :w] * _sigmoid(gl[:, w:])
    y_ref[...] = out.reshape(y_ref.shape)


def _s5(u3, bbar, a8, cmat, dskip, wglu):
    lp, bsz, w = u3.shape
    tt = 128
    nst2 = a8.shape[1]
    const = lambda bg, i: (0, 0)
    return pl.pallas_call(
        _s5_kernel,
        grid=(bsz // 8, lp // tt),
        in_specs=[pl.BlockSpec((tt, 8, w), lambda bg, i: (i, bg, 0)),
                  pl.BlockSpec(bbar.shape, const),
                  pl.BlockSpec(a8.shape, const),
                  pl.BlockSpec(cmat.shape, const),
                  pl.BlockSpec(dskip.shape, const),
                  pl.BlockSpec(wglu.shape, const)],
        out_specs=pl.BlockSpec((tt, 8, w), lambda bg, i: (i, bg, 0)),
        out_shape=jax.ShapeDtypeStruct((lp, bsz, w), F32),
        scratch_shapes=[pltpu.VMEM((tt * 8, nst2), F32), pltpu.VMEM((8, nst2), F32)],
        compiler_params=_params(("parallel", "arbitrary")),
        name="s5_scan",
    )(u3, bbar, a8, cmat, dskip, wglu)


_LEVEL_HALVES = (32, 16, 8, 4, 2, 1)


def _hgrn_constants():
    c = HGRN_CHUNK
    t = np.arange(c)[:, None]
    j = np.arange(c)[None, :]
    mats = [(j <= t), (j > t)]
    for half in _LEVEL_HALVES:
        upper = (t & half) != 0
        seg_lo = t - (t % half)
        seg_hi = t | (half - 1)
        mats.append(np.where(upper, (j >= seg_lo) & (j <= t), (j > t) & (j <= seg_hi)))
    dmat = np.concatenate(mats, axis=0).astype(np.float32)
    s = np.arange(4 * c)[None, :] % c
    x = t ^ s
    hb = np.floor(np.log2(np.maximum(x, 1))).astype(np.int32)
    lv = np.where(s > t, -1, np.where(s == t, 6, hb)).astype(np.int32)
    hm = (np.arange(4 * c)[:, None] // c == np.arange(4 * c)[None, :] // c).astype(np.float32)
    return dmat, lv, hm


def _hgrn_kernel(zq_ref, zf_ref, zi_ref, zg_ref, lb_ref, ng_ref, dmat_ref, lv_ref, hm_ref,
                 o_ref, st_ref):
    c = HGRN_CHUNK
    w = zq_ref.shape[2]
    nchunks = zq_ref.shape[1] // c
    st_ref[...] = jnp.zeros_like(st_ref)

    lb = lb_ref[...]
    lbm = jnp.maximum(lb, LB_FLOOR)
    one_m_lb = 1.0 - lb
    k_off = lb - lbm
    ng = ng_ref[...]
    row = lax.broadcasted_iota(jnp.int32, (c, w), 0)

    def chunk(ci, carry):
        r0 = pl.multiple_of(ci * c, c)
        zq = zq_ref[0, pl.ds(r0, c), :]
        zf = zf_ref[0, pl.ds(r0, c), :]
        v = zi_ref[0, pl.ds(r0, c), :]
        zg = zg_ref[0, pl.ds(r0, c), :]
        hm = hm_ref[...]
        hmb = hm.astype(BF16)
        lv = lv_ref[...]

        q = zq * _sigmoid(zq)
        logf = jnp.log(lbm + one_m_lb * _sigmoid(zf))
        k = one_m_lb * _sigmoid(-zf) + k_off

        dmat = dmat_ref[...]
        hi, mid, lo = _split3(logf)
        e_all = jnp.exp(_dot(dmat, hi) + _dot(dmat, mid) + _dot(dmat, lo))
        e_cum = e_all[0:c]
        e_rest = e_all[c:2 * c]

        p = jnp.zeros((c, 4 * c), F32)
        for li, half in enumerate(_LEVEL_HALVES):
            e_l = e_all[(2 + li) * c:(3 + li) * c]
            upper = (row & half) != 0
            q_l = jnp.where(upper, q * e_l, 0.0).astype(BF16)
            k_l = jnp.where(upper, 0.0, k * e_l)
            r_l = (jnp.concatenate([k_l] * HGRN_HEADS, axis=0) * hm).astype(BF16)
            s_l = _dot_nt(q_l, r_l)
            p = jnp.where(lv == int(math.log2(half)), s_l, p)
        s_d = _dot((q * k).astype(BF16), hmb)
        p = jnp.where(lv == 6, s_d, p)

        vbd = (jnp.concatenate([v] * HGRN_HEADS, axis=0) * hm).astype(BF16)
        st = st_ref[...]
        o = _dot(p.astype(BF16), vbd) + _dot_nt((q * e_cum).astype(BF16), st.astype(BF16))

        upd = _dot_tn(v.astype(BF16), (k * e_rest).astype(BF16))
        st_ref[...] = st * e_cum[c - 1:c, :] + upd * hm

        oo_hi, oo_lo = _split2(o * o)
        ms = (_dot(oo_hi, hmb) + _dot(oo_lo, hmb)) * (1.0 / (w // HGRN_HEADS))
        out = o * lax.rsqrt(ms + RMS_EPS) * ng * (zg * _sigmoid(zg))
        o_ref[0, pl.ds(r0, c), :] = out
        return carry

    lax.fori_loop(0, nchunks, chunk, 0)


def _hgrn(zq, zf, zi, zg, lb, ng):
    bsz, lp, w = zq.shape
    dmat, lv, hm = _hgrn_constants()
    dmat = jnp.asarray(dmat, BF16)
    lv = jnp.asarray(lv)
    hm = jnp.asarray(hm, F32)
    seq = pl.BlockSpec((1, lp, w), lambda b: (b, 0, 0))
    const = lambda b: (0, 0)
    return pl.pallas_call(
        _hgrn_kernel,
        grid=(bsz,),
        in_specs=[seq, seq, seq, seq,
                  pl.BlockSpec(lb.shape, const), pl.BlockSpec(ng.shape, const),
                  pl.BlockSpec(dmat.shape, const), pl.BlockSpec(lv.shape, const),
                  pl.BlockSpec(hm.shape, const)],
        out_specs=seq,
        out_shape=jax.ShapeDtypeStruct((bsz, lp, w), F32),
        scratch_shapes=[pltpu.VMEM((w, w), F32)],
        compiler_params=_params(("parallel",)),
        name="hgrn2",
    )(zq, zf, zi, zg, lb, ng, dmat, lv, hm)


def _attn_kernel(q_ref, k1_ref, k2_ref, v_ref, lam_ref, g_ref, o_ref, m_ref, l_ref, acc_ref,
                 *, tq, tk, out_scale):
    i = pl.program_id(1)
    lp = k1_ref.shape[1]
    q0 = i * tq
    nkv = (q0 + tq + tk - 1) // tk
    n_full = jnp.maximum(q0 // tk, 1)
    kloc = lax.broadcasted_iota(jnp.int32, (tq, tk), 1)
    k_minus_q = kloc - lax.broadcasted_iota(jnp.int32, (tq, tk), 0)
    nrep = tk // 128
    lam = lam_ref[...]
    g = g_ref[...]

    for hd in range(ATTN_HEADS):
        cs = slice(hd * 128, (hd + 1) * 128)
        qh = q_ref[0, :, cs]
        m_ref[...] = jnp.full(m_ref.shape, MASK_VALUE, F32)
        l_ref[...] = jnp.zeros(l_ref.shape, F32)
        acc_ref[...] = jnp.zeros(acc_ref.shape, F32)

        def block(j, masked):
            k0 = pl.multiple_of(jnp.minimum(j * tk, lp - tk), 128)
            vb = v_ref[0, pl.ds(k0, tk), cs]
            if masked:
                valid = (k_minus_q <= q0 - k0) & (kloc >= jnp.maximum(PAD, j * tk) - k0)
            for c, kr in enumerate((k1_ref, k2_ref)):
                s = _dot_nt(qh, kr[0, pl.ds(k0, tk), cs])
                if masked:
                    s = jnp.where(valid, s, MASK_VALUE)
                m_prev = m_ref[c]
                m_next = jnp.maximum(m_prev, jnp.max(s, axis=-1, keepdims=True))
                alpha = jnp.exp2(m_prev - m_next)
                p = jnp.exp2(s - jnp.concatenate([m_next] * nrep, axis=1))
                psum = p[:, 0:128]
                for r in range(1, nrep):
                    psum = psum + p[:, r * 128:(r + 1) * 128]
                m_ref[c] = m_next
                l_ref[c] = alpha * l_ref[c] + psum
                acc_ref[c] = alpha * acc_ref[c] + _dot(p.astype(BF16), vb)

        def run(masked):
            def body(j, carry):
                block(j, masked)
                return carry
            return body

        block(0, True)
        lax.fori_loop(1, n_full, run(False), 0)
        lax.fori_loop(n_full, nkv, run(True), 0)

        l1 = jnp.sum(l_ref[0], axis=-1, keepdims=True)
        l2 = jnp.sum(l_ref[1], axis=-1, keepdims=True)
        o = acc_ref[0] / l1 - lam * (acc_ref[1] / l2)
        ms = jnp.mean(o * o, axis=-1, keepdims=True)
        o_ref[0, :, cs] = (o * lax.rsqrt(ms + RMS_EPS) * g * out_scale).astype(o_ref.dtype)


def _attn(q, k1, k2, v, lam_row, g_row, out_scale):
    bsz, lp, w = q.shape
    tq = _row_tile(lp, 544)
    tk = 256
    qs = pl.BlockSpec((1, tq, w), lambda b, i: (b, i, 0))
    full = pl.BlockSpec((1, lp, w), lambda b, i: (b, 0, 0))
    row = pl.BlockSpec((1, 128), lambda b, i: (0, 0))
    return pl.pallas_call(
        functools.partial(_attn_kernel, tq=tq, tk=tk, out_scale=out_scale),
        grid=(bsz, lp // tq),
        in_specs=[qs, full, full, full, row, row],
        out_specs=qs,
        out_shape=jax.ShapeDtypeStruct((bsz, lp, w), BF16),
        scratch_shapes=[pltpu.VMEM((2, tq, 128), F32), pltpu.VMEM((2, tq, 128), F32),
                        pltpu.VMEM((2, tq, 128), F32)],
        compiler_params=_params(("parallel", "arbitrary")),
        name="diff_attn",
    )(q, k1, k2, v, lam_row, g_row)


def _router_logits(hn, wr_hi_ref, wr_lo_ref, br_ref):
    x_hi, x_lo = _split2(hn)
    return (_dot(x_hi, wr_hi_ref[...]) + _dot(x_hi, wr_lo_ref[...]) + _dot(x_lo, wr_hi_ref[...])
            + br_ref[...])


def _router_group(logits):
    lane = lax.broadcasted_iota(jnp.int32, logits.shape, 1)
    is_g = lane < MOE_GROUPS
    glog = jnp.where(is_g, logits, MASK_VALUE)
    gmax = jnp.max(glog, axis=-1, keepdims=True)
    gsum = jnp.sum(jnp.where(is_g, jnp.exp(glog - gmax), 0.0), axis=-1, keepdims=True)
    gi = jnp.min(jnp.where(is_g & (glog == gmax), lane.astype(F32), 1e9), axis=-1, keepdims=True)
    return gi, 1.0 / gsum


def _router_gates(logits, gi, gp):
    lane = lax.broadcasted_iota(jnp.int32, logits.shape, 1)
    lanef = lane.astype(F32)
    big = 1e9
    e_idx = lane - MOE_GROUPS
    in_group = (e_idx >= 0) & (e_idx < MOE_EXPERTS) & ((e_idx >> 2) == gi)
    elog = jnp.where(in_group, logits, MASK_VALUE)
    v1 = jnp.max(elog, axis=-1, keepdims=True)
    i1 = jnp.min(jnp.where(in_group & (elog == v1), lanef, big), axis=-1, keepdims=True)
    rest = in_group & (lanef != i1)
    elog2 = jnp.where(rest, logits, MASK_VALUE)
    v2 = jnp.max(elog2, axis=-1, keepdims=True)
    i2 = jnp.min(jnp.where(rest & (elog2 == v2), lanef, big), axis=-1, keepdims=True)
    e21 = jnp.exp(v2 - v1)
    den = 1.0 / (1.0 + e21)
    return jnp.where(lanef == i1, gp * den, 0.0) + jnp.where(lanef == i2, gp * e21 * den, 0.0)


def _out_proj_kernel(h_ref, ys_ref, yr_ref, ya_ref, wo_ref, gn_ref, wr_hi_ref, wr_lo_ref, br_ref,
                     hout_ref, gi_ref, *, tl):
    acc = _dot(ys_ref[...].astype(BF16), wo_ref[0:256, :])
    acc += _dot(yr_ref[0].astype(BF16), wo_ref[256:512, :])
    acc += _dot(ya_ref[0], wo_ref[512:1024, :])
    h = _load_tokens(h_ref, tl) + acc
    rowpos = pl.program_id(1) * tl + lax.broadcasted_iota(jnp.int32, h.shape, 0)
    h = jnp.where(rowpos >= PAD, h, 0.0)
    _store_tokens(hout_ref, h)
    ms = jnp.mean(h * h, axis=-1, keepdims=True)
    hn = h * lax.rsqrt(ms + RMS_EPS) * gn_ref[...]
    gi, _ = _router_group(_router_logits(hn, wr_hi_ref, wr_lo_ref, br_ref))
    gi_ref[0] = gi.astype(jnp.int32)


def _out_proj(h_slab, bsz, lp, ys2d, yr, ya, wo_bf, gn, wr_hi, wr_lo, br):
    d = SLAB * 128
    tl = _row_tile(lp, 544)
    nl = lp // tl
    row = lambda b, i: (b, i, 0)
    const = lambda b, i: (0, 0)
    slab = pl.BlockSpec((tl * SLAB, 128), lambda b, i: (b * nl + i, 0))
    return pl.pallas_call(
        functools.partial(_out_proj_kernel, tl=tl),
        grid=(bsz, nl),
        in_specs=[slab,
                  pl.BlockSpec((tl, 256), lambda b, i: (i, b)),
                  pl.BlockSpec((1, tl, 256), row),
                  pl.BlockSpec((1, tl, 512), row),
                  pl.BlockSpec(wo_bf.shape, const),
                  pl.BlockSpec(gn.shape, const),
                  pl.BlockSpec(wr_hi.shape, const),
                  pl.BlockSpec(wr_lo.shape, const),
                  pl.BlockSpec(br.shape, const)],
        out_specs=(slab, pl.BlockSpec((1, tl, 1), row)),
        out_shape=(jax.ShapeDtypeStruct((bsz * lp * SLAB, 128), F32),
                   jax.ShapeDtypeStruct((bsz, lp, 1), jnp.int32)),
        compiler_params=_params(("parallel", "parallel")),
        name="out_proj_router",
    )(h_slab, ys2d, yr, ya, wo_bf, gn, wr_hi, wr_lo, br)


def _moe_kernel(grp_ref, used_ref, src_ref, srcn_ref, dst_ref, h_hbm, gn_ref, wr_hi_ref, wr_lo_ref, br_ref,
                wg_ref, wu_ref, wd_ref, out_hbm, xbuf, ybuf, gsem, ssem, *, tm):
    p = pl.program_id(0)
    used = used_ref[0]
    g = grp_ref[p]
    slot = p % 2

    def gather(idx_ref, into):
        def body(r, carry):
            rows = pl.ds(pl.multiple_of((into * tm + r) * SLAB, SLAB), SLAB)
            pltpu.make_async_copy(h_hbm.at[idx_ref[0, 0, r]], xbuf.at[rows, :], gsem.at[into]).start()
            return carry
        lax.fori_loop(0, tm, body, 0)

    def scatter(dst_of_row):
        def body(r, carry):
            rows = pl.ds(pl.multiple_of(r * SLAB, SLAB), SLAB)
            pltpu.make_async_copy(ybuf.at[rows, :], out_hbm.at[dst_of_row(r)], ssem.at[0]).start()
            return carry
        lax.fori_loop(0, tm, body, 0)

    def wait_tile(sem):
        pltpu.make_async_copy(h_hbm.at[pl.ds(0, tm)], out_hbm.at[pl.ds(0, tm)], sem).wait()

    @pl.when(p == 0)
    def _():
        gather(src_ref, 0)
        spare = out_hbm.shape[0] - tm
        ybuf[...] = jnp.zeros_like(ybuf)
        scatter(lambda r: spare + r)
        wait_tile(ssem.at[0])

    @pl.when(p < used)
    def _():
        wait_tile(gsem.at[slot])

        @pl.when(p + 1 < used)
        def _():
            gather(srcn_ref, 1 - slot)

        h = _load_tokens(xbuf, tm, first=slot * tm)
        ms = jnp.mean(h * h, axis=-1, keepdims=True)
        hn = h * lax.rsqrt(ms + RMS_EPS) * gn_ref[...]
        logits = _router_logits(hn, wr_hi_ref, wr_lo_ref, br_ref)
        lane = lax.broadcasted_iota(jnp.int32, logits.shape, 1)
        is_g = lane < MOE_GROUPS
        glog = jnp.where(is_g, logits, MASK_VALUE)
        gmax = jnp.max(glog, axis=-1, keepdims=True)
        gsum = jnp.sum(jnp.where(is_g, jnp.exp(glog - gmax), 0.0), axis=-1, keepdims=True)
        lg = jnp.sum(jnp.where(lane == g, logits, 0.0), axis=-1, keepdims=True)
        gates = _router_gates(logits, g, jnp.exp(lg - gmax) / gsum)

        x = hn.astype(BF16)
        y = h
        for e in range(MOE_EXPERTS_PER_GROUP):
            ge = jnp.sum(jnp.where(lane == MOE_GROUPS + MOE_EXPERTS_PER_GROUP * g + e, gates, 0.0),
                         axis=-1, keepdims=True)
            a = _dot(x, wg_ref[e])
            b = _dot(x, wu_ref[e])
            he = (a * _sigmoid(a)) * b * ge
            y = y + _dot(he.astype(BF16), wd_ref[e])

        @pl.when(p > 0)
        def _():
            wait_tile(ssem.at[0])

        _store_tokens(ybuf, y)
        scatter(lambda r: dst_ref[0, 0, r])

        @pl.when(p + 1 == used)
        def _():
            wait_tile(ssem.at[0])


def _moe_routed(h_slab, gi, gn, wr_hi, wr_lo, br, wg, wu, wd, tm=512):
    t = gi.shape[0]
    ne, d, ff = wg.shape
    tm = _row_tile(t, tm)
    nt = t // tm + MOE_GROUPS

    order = jnp.argsort(gi, stable=True).astype(jnp.int32)
    counts = jnp.sum((gi[:, None] == jnp.arange(MOE_GROUPS, dtype=jnp.int32)[None, :]).astype(jnp.int32),
                     axis=0)
    starts = jnp.cumsum(counts) - counts
    tiles_g = (counts + tm - 1) // tm
    tile_end = jnp.cumsum(tiles_g)
    tile_start = tile_end - tiles_g
    pidx = jnp.arange(nt, dtype=jnp.int32)
    grp = jnp.minimum(jnp.sum((pidx[:, None] >= tile_end[None, :]).astype(jnp.int32), axis=1),
                      MOE_GROUPS - 1)
    j = pidx - tile_start[grp]
    nval = jnp.where(pidx < tile_end[-1], jnp.clip(counts[grp] - j * tm, 0, tm), 0)
    lane = jnp.arange(tm, dtype=jnp.int32)[None, :]
    rows = starts[grp][:, None] + j[:, None] * tm + lane
    src = order[jnp.clip(rows, 0, t - 1)]
    dst = jnp.where(lane < nval[:, None], src, t + lane)
    src = src.reshape(nt, 1, tm)
    dst = dst.reshape(nt, 1, tm).astype(jnp.int32)
    used = tile_end[-1:].astype(jnp.int32)

    const2 = lambda p, grp, used: (0, 0)
    wspec = lambda shape: pl.BlockSpec((MOE_EXPERTS_PER_GROUP,) + shape, lambda p, grp, used: (grp[p], 0, 0))
    idx_spec = lambda shift: pl.BlockSpec(
        (1, 1, tm), lambda p, grp, used: (jnp.minimum(p + shift, nt - 1), 0, 0), memory_space=pltpu.SMEM)
    grid_spec = pltpu.PrefetchScalarGridSpec(
        num_scalar_prefetch=2,
        grid=(nt,),
        in_specs=[idx_spec(0), idx_spec(1), idx_spec(0),
                  pl.BlockSpec(memory_space=pl.ANY),
                  pl.BlockSpec(gn.shape, const2),
                  pl.BlockSpec(wr_hi.shape, const2),
                  pl.BlockSpec(wr_lo.shape, const2),
                  pl.BlockSpec(br.shape, const2),
                  wspec((d, ff)), wspec((d, ff)), wspec((ff, d))],
        out_specs=pl.BlockSpec(memory_space=pl.ANY),
        scratch_shapes=[pltpu.VMEM((2 * tm * SLAB, 128), F32), pltpu.VMEM((tm * SLAB, 128), F32),
                        pltpu.SemaphoreType.DMA((2,)), pltpu.SemaphoreType.DMA((1,))],
    )
    n_in = h_slab.shape[0] // SLAB
    out = pl.pallas_call(
        functools.partial(_moe_kernel, tm=tm),
        grid_spec=grid_spec,
        out_shape=jax.ShapeDtypeStruct((t + tm, SLAB, 128), F32),
        compiler_params=_params(("arbitrary",)),
        name="moe_experts",
    )(grp, used, src, src, dst, h_slab.reshape(n_in, SLAB, 128), gn, wr_hi, wr_lo, br, wg, wu, wd)
    return out.reshape((t + tm) * SLAB, 128)


def _final_norm_kernel(h_ref, g_ref, o_ref, *, skip, chunk):
    g = g_ref[...]
    for c in range(o_ref.shape[1] // chunk):
        x = _load_tokens(h_ref, chunk, first=skip + c * chunk)
        ms = jnp.mean(x * x, axis=-1, keepdims=True)
        o_ref[0, c * chunk:(c + 1) * chunk, :] = x * lax.rsqrt(ms + RMS_EPS) * g


def _final_norm(h_slab, bsz, lp, g, seq):
    d = SLAB * 128
    chunk = _row_tile(seq, 512)
    return pl.pallas_call(
        functools.partial(_final_norm_kernel, skip=lp - seq, chunk=chunk),
        grid=(bsz,),
        in_specs=[pl.BlockSpec((lp * SLAB, 128), lambda b: (b, 0)),
                  pl.BlockSpec((1, d), lambda b: (0, 0))],
        out_specs=pl.BlockSpec((1, seq, d), lambda b: (b, 0, 0)),
        out_shape=jax.ShapeDtypeStruct((bsz, seq, d), F32),
        compiler_params=_params(("parallel",)),
        name="final_norm",
    )(h_slab, g.reshape(1, d))


def _s5_tables(lam_re, lam_im, log_dt, b_re, b_im, c_re, c_im):
    ng, ns = lam_re.shape
    lr = lam_re.astype(F32)
    li = lam_im.astype(F32)
    dt = jnp.exp(log_dt.astype(F32))[:, None]
    mag = jnp.exp(lr * dt)
    abar_r = mag * jnp.cos(li * dt)
    abar_i = mag * jnp.sin(li * dt)
    den = lr * lr + li * li
    zr = abar_r - 1.0
    zi = abar_i
    fr = (zr * lr + zi * li) / den
    fi = (zi * lr - zr * li) / den
    br = b_re.astype(F32)
    bi = b_im.astype(F32)
    bb_r = fr[..., None] * br - fi[..., None] * bi
    bb_i = fr[..., None] * bi + fi[..., None] * br
    eye = jnp.eye(ng, dtype=F32)
    nch = b_re.shape[2]
    to_b = lambda m: jnp.einsum('gnc,gh->gchn', m, eye).reshape(ng * nch, ng * ns)
    bbar = jnp.concatenate([to_b(bb_r), to_b(bb_i)], axis=1).astype(BF16)
    to_c = lambda m: jnp.einsum('gcn,gh->gnhc', m.astype(F32), eye).reshape(ng * ns, ng * nch)
    cmat = jnp.concatenate([to_c(c_re), -to_c(c_im)], axis=0).astype(BF16)
    a_row = jnp.concatenate([abar_r.reshape(1, -1), abar_i.reshape(1, -1)], axis=1)
    a8 = jnp.broadcast_to(a_row, (8, a_row.shape[1]))
    return bbar, a8, cmat


def _rope_tables(lp):
    half = 32
    inv_freq = 1.0 / (ROPE_THETA ** (jnp.arange(0, 2 * half, 2, dtype=F32) / (2 * half)))
    pos = jnp.arange(lp, dtype=F32) - float(PAD)
    ang = pos[:, None] * inv_freq[None, :]
    ang = jnp.concatenate([ang, ang, ang, ang], axis=-1)
    cos = jnp.cos(ang)
    sin = jnp.sin(ang)
    first = (jnp.arange(128) % 64) < half
    sina = jnp.where(first[None, :], -sin, 0.0)
    sinb = jnp.where(first[None, :], 0.0, sin)
    return cos, sina, sinb


def kernel(x, meta_tokens, norm_mix_g, w_in, s5_lambda_re, s5_lambda_im, s5_log_dt, s5_b_re, s5_b_im, s5_c_re, s5_c_im, s5_d, s5_w_glu, hgrn_lower_bounds, hgrn_norm_g, diff_lambda_q1, diff_lambda_k1, diff_lambda_q2, diff_lambda_k2, diff_subln_g, w_out, norm_ffn_g, moe_w_group, moe_b_group, moe_w_expert, moe_b_expert, moe_w_gate, moe_w_up, moe_w_down, final_norm_g):
    bsz, seq, d = x.shape
    depth = w_in.shape[0]
    lp = PAD + N_META + seq
    assert lp % 128 == 0 and bsz % 8 == 0

    meta = jnp.broadcast_to(meta_tokens.astype(F32)[None], (bsz, N_META, d))
    h = jnp.concatenate([jnp.zeros((bsz, PAD, d), F32), meta, x.astype(F32)], axis=1)
    assert d == SLAB * 128
    h = h.reshape(bsz * lp * SLAB, 128)

    cos, sina, sinb = _rope_tables(lp)
    lb_w = jax.nn.softmax(hgrn_lower_bounds.astype(F32), axis=0)
    lower_bounds = jnp.cumsum(lb_w, axis=0) - lb_w[0:1]

    for layer in range(depth):
        lam_init = 0.8 - 0.6 * math.exp(-0.3 * layer)
        u2d, hq, hf, hi, hg, aq, ak1, ak2, av = _in_proj(
            h, bsz, lp, norm_mix_g[layer].reshape(1, d).astype(F32), w_in[layer].astype(BF16), cos, sina, sinb)

        bbar, a8, cmat = _s5_tables(s5_lambda_re[layer], s5_lambda_im[layer], s5_log_dt[layer],
                                    s5_b_re[layer], s5_b_im[layer], s5_c_re[layer], s5_c_im[layer])
        y_ssm = _s5(u2d.reshape(lp, bsz, 256), bbar, a8, cmat,
                    s5_d[layer].reshape(1, -1).astype(F32), s5_w_glu[layer].astype(BF16))

        ng = jnp.tile(hgrn_norm_g[layer].astype(F32), HGRN_HEADS).reshape(1, -1)
        y_rec = _hgrn(hq, hf, hi, hg, lower_bounds[layer].reshape(1, -1), ng)

        lam = (jnp.exp(jnp.sum(diff_lambda_q1[layer].astype(F32) * diff_lambda_k1[layer].astype(F32)))
               - jnp.exp(jnp.sum(diff_lambda_q2[layer].astype(F32) * diff_lambda_k2[layer].astype(F32)))
               + lam_init)
        y_att = _attn(aq, ak1, ak2, av, jnp.full((1, 128), lam, F32),
                      diff_subln_g[layer].reshape(1, -1).astype(F32), 1.0 - lam_init)

        wr = jnp.concatenate([moe_w_group[layer].astype(F32), moe_w_expert[layer].astype(F32)], axis=1)
        wr = jnp.pad(wr, ((0, 0), (0, ROUTER_LANES - wr.shape[1])))
        wr_hi = wr.astype(BF16)
        wr_lo = (wr - wr_hi.astype(F32)).astype(BF16)
        br = jnp.concatenate([moe_b_group[layer].astype(F32), moe_b_expert[layer].astype(F32)])
        br = jnp.pad(br, (0, ROUTER_LANES - br.shape[0])).reshape(1, -1)
        gn = norm_ffn_g[layer].reshape(1, d).astype(F32)
        h, gi = _out_proj(h, bsz, lp, y_ssm.reshape(lp, bsz * 256), y_rec, y_att,
                          w_out[layer].astype(BF16), gn, wr_hi, wr_lo, br)
        h = _moe_routed(h, gi.reshape(bsz * lp), gn, wr_hi, wr_lo, br, moe_w_gate[layer].astype(BF16),
                        moe_w_up[layer].astype(BF16), moe_w_down[layer].astype(BF16))

    return _final_norm(h, bsz, lp, final_norm_g.astype(F32), seq)
```

```python
import functools
import math

import numpy as np
import jax
import jax.numpy as jnp
from jax import lax
from jax.experimental import pallas as pl
from jax.experimental.pallas import tpu as pltpu

F32 = jnp.float32
BF16 = jnp.bfloat16

N_META = 16
PAD = 112
RMS_EPS = 1e-6
MASK_VALUE = -1e30
LB_FLOOR = 1e-30
ROPE_THETA = 10000.0

S5_GROUP_CH = 16
S5_STATE = 64
HGRN_HEADS = 4
HGRN_CHUNK = 64
ATTN_HEADS = 4
ATTN_QK_DIM = 64
Q_SCALE = ATTN_QK_DIM ** -0.5 * math.log2(math.e)
MOE_GROUPS = 4
MOE_EXPERTS_PER_GROUP = 4
MOE_EXPERTS = MOE_GROUPS * MOE_EXPERTS_PER_GROUP
ROUTER_LANES = 128
ROW_UNROLL = 8

VMEM_LIMIT = 56 * 1024 * 1024


def _params(sem, vmem=VMEM_LIMIT):
    return pltpu.CompilerParams(dimension_semantics=sem, vmem_limit_bytes=vmem)


def _row_tile(n, cap, mult=16):
    best = None
    for t in range(mult, min(n, cap) + 1, mult):
        if n % t == 0:
            best = t
    assert best is not None, (n, cap)
    return best


def _sigmoid(x):
    return 1.0 / (1.0 + jnp.exp(-x))


def _dot(a, b):
    return jnp.dot(a, b, preferred_element_type=F32)


def _dot_nt(a, b):
    return lax.dot_general(a, b, (((1,), (1,)), ((), ())), preferred_element_type=F32)


def _dot_tn(a, b):
    return lax.dot_general(a, b, (((0,), (0,)), ((), ())), preferred_element_type=F32)


SLAB = 8


def _load_tokens(ref, n, first=0):
    return jnp.concatenate([ref[pl.ds(first * SLAB + s, n, stride=SLAB), :] for s in range(SLAB)], axis=1)


def _store_tokens(ref, val, first=0):
    n = val.shape[0]
    for s in range(SLAB):
        ref[pl.ds(first * SLAB + s, n, stride=SLAB), :] = val[:, s * 128:(s + 1) * 128]


def _embed_kernel(x_ref, meta_ref, o_ref, *, chunk):
    seq, d = x_ref.shape[1], x_ref.shape[2]
    _store_tokens(o_ref, jnp.zeros((PAD, d), F32))
    _store_tokens(o_ref, meta_ref[...], first=PAD)
    for c in range(seq // chunk):
        _store_tokens(o_ref, x_ref[0, c * chunk:(c + 1) * chunk, :], first=PAD + N_META + c * chunk)


def _embed(x, meta):
    bsz, seq, d = x.shape
    lp = PAD + N_META + seq
    return pl.pallas_call(
        functools.partial(_embed_kernel, chunk=_row_tile(seq, 512)),
        grid=(bsz,),
        in_specs=[pl.BlockSpec((1, seq, d), lambda b: (b, 0, 0)),
                  pl.BlockSpec(meta.shape, lambda b: (0, 0))],
        out_specs=pl.BlockSpec((lp * SLAB, 128), lambda b: (b, 0)),
        out_shape=jax.ShapeDtypeStruct((bsz * lp * SLAB, 128), F32),
        compiler_params=_params(("parallel",)),
        name="embed",
    )(x, meta)


def _split2(x):
    hi = x.astype(BF16)
    lo = (x - hi.astype(F32)).astype(BF16)
    return hi, lo


def _split3(x):
    hi = x.astype(BF16)
    r = x - hi.astype(F32)
    mid = r.astype(BF16)
    lo = (r - mid.astype(F32)).astype(BF16)
    return hi, mid, lo


def _in_proj_kernel(h_ref, g_ref, w_ref, cos_ref, sina_ref, sinb_ref,
                    u_ref, hq_ref, hf_ref, hi_ref, hg_ref, q_ref, k1_ref, k2_ref, v_ref):
    x = _load_tokens(h_ref, u_ref.shape[0])
    ms = jnp.mean(x * x, axis=-1, keepdims=True)
    xn = (x * lax.rsqrt(ms + RMS_EPS) * g_ref[...]).astype(BF16)

    def sec(lo, hi):
        return _dot(xn, w_ref[:, lo:hi])

    u_ref[...] = sec(0, 256)
    hq_ref[0] = sec(256, 512)
    hf_ref[0] = sec(512, 768)
    hi_ref[0] = sec(768, 1024)
    hg_ref[0] = sec(1024, 1280)

    cos = cos_ref[...]
    sina = sina_ref[...]
    sinb = sinb_ref[...]
    lane = lax.broadcasted_iota(jnp.int32, cos.shape, 1)
    first = lane < 64

    def rope(t):
        return t * cos + pltpu.roll(t, 96, 1) * sina + pltpu.roll(t, 32, 1) * sinb

    for hd in range(ATTN_HEADS):
        c0 = 1280 + hd * 128
        q = rope(sec(c0, c0 + 128)) * Q_SCALE
        q_ref[0, :, hd * 128:(hd + 1) * 128] = q.astype(BF16)
        c0 = 1792 + hd * 128
        k = rope(sec(c0, c0 + 128))
        k1_ref[0, :, hd * 128:(hd + 1) * 128] = jnp.where(first, k, 0.0).astype(BF16)
        k2_ref[0, :, hd * 128:(hd + 1) * 128] = jnp.where(first, 0.0, k).astype(BF16)
    v_ref[0] = sec(2304, 2816).astype(BF16)


def _in_proj(h_slab, bsz, lp, g, w_bf, cos, sina, sinb):
    d = SLAB * 128
    tl = _row_tile(lp, 544)
    nl = lp // tl
    row = lambda b, i: (b, i, 0)
    tab = pl.BlockSpec((tl, 128), lambda b, i: (i, 0))
    out_shape = (
        jax.ShapeDtypeStruct((lp, bsz * 256), F32),
        jax.ShapeDtypeStruct((bsz, lp, 256), F32),
        jax.ShapeDtypeStruct((bsz, lp, 256), F32),
        jax.ShapeDtypeStruct((bsz, lp, 256), F32),
        jax.ShapeDtypeStruct((bsz, lp, 256), F32),
        jax.ShapeDtypeStruct((bsz, lp, 512), BF16),
        jax.ShapeDtypeStruct((bsz, lp, 512), BF16),
        jax.ShapeDtypeStruct((bsz, lp, 512), BF16),
        jax.ShapeDtypeStruct((bsz, lp, 512), BF16),
    )
    s256 = pl.BlockSpec((1, tl, 256), row)
    s512 = pl.BlockSpec((1, tl, 512), row)
    return pl.pallas_call(
        _in_proj_kernel,
        grid=(bsz, nl),
        in_specs=[pl.BlockSpec((tl * SLAB, 128), lambda b, i: (b * nl + i, 0)),
                  pl.BlockSpec((1, d), lambda b, i: (0, 0)),
                  pl.BlockSpec(w_bf.shape, lambda b, i: (0, 0)),
                  tab, tab, tab],
        out_specs=(pl.BlockSpec((tl, 256), lambda b, i: (i, b)), s256, s256, s256, s256,
                   s512, s512, s512, s512),
        out_shape=out_shape,
        compiler_params=_params(("parallel", "parallel")),
        name="in_proj",
    )(h_slab, g, w_bf, cos, sina, sinb)


def _s5_kernel(u_ref, bbar_ref, a_ref, cmat_ref, d_ref, wglu_ref, y_ref, xs_ref, st_ref):
    tt = u_ref.shape[0]
    nst = a_ref.shape[1] // 2

    @pl.when(pl.program_id(1) == 0)
    def _():
        st_ref[...] = jnp.zeros_like(st_ref)

    u2 = u_ref[...].reshape(tt * 8, u_ref.shape[2])
    xs_ref[...] = _dot(u2.astype(BF16), bbar_ref[...])

    ar = a_ref[:, :nst]
    ai = a_ref[:, nst:]

    def step(t, carry):
        sr, si = carry
        r0 = pl.multiple_of(t * 8, 8)
        xr = xs_ref[pl.ds(r0, 8), :nst]
        xi = xs_ref[pl.ds(r0, 8), nst:]
        nr = ar * sr - ai * si + xr
        ni = ar * si + ai * sr + xi
        xs_ref[pl.ds(r0, 8), :nst] = nr
        xs_ref[pl.ds(r0, 8), nst:] = ni
        return nr, ni

    sr, si = lax.fori_loop(0, tt, step, (st_ref[:, :nst], st_ref[:, nst:]))
    st_ref[:, :nst] = sr
    st_ref[:, nst:] = si

    y = _dot(xs_ref[...].astype(BF16), cmat_ref[...]) + d_ref[...] * u2
    g = 0.5 * y * (1.0 + jnp.tanh(0.7978845608028654 * (y + 0.044715 * (y * y * y))))
    gl = _dot(g.astype(BF16), wglu_ref[...])
    w = gl.shape[1] // 2
    out = gl[:, :w] * _sigmoid(gl[:, w:])
    y_ref[...] = out.reshape(y_ref.shape)


def _s5(u3, bbar, a8, cmat, dskip, wglu):
    lp, bsz, w = u3.shape
    tt = 128
    nst2 = a8.shape[1]
    const = lambda bg, i: (0, 0)
    return pl.pallas_call(
        _s5_kernel,
        grid=(bsz // 8, lp // tt),
        in_specs=[pl.BlockSpec((tt, 8, w), lambda bg, i: (i, bg, 0)),
                  pl.BlockSpec(bbar.shape, const),
                  pl.BlockSpec(a8.shape, const),
                  pl.BlockSpec(cmat.shape, const),
                  pl.BlockSpec(dskip.shape, const),
                  pl.BlockSpec(wglu.shape, const)],
        out_specs=pl.BlockSpec((tt, 8, w), lambda bg, i: (i, bg, 0)),
        out_shape=jax.ShapeDtypeStruct((lp, bsz, w), F32),
        scratch_shapes=[pltpu.VMEM((tt * 8, nst2), F32), pltpu.VMEM((8, nst2), F32)],
        compiler_params=_params(("parallel", "arbitrary")),
        name="s5_scan",
    )(u3, bbar, a8, cmat, dskip, wglu)


_LEVEL_HALVES = (32, 16, 8, 4, 2, 1)


def _hgrn_constants():
    c = HGRN_CHUNK
    t = np.arange(c)[:, None]
    j = np.arange(c)[None, :]
    mats = [(j <= t), (j > t)]
    for half in _LEVEL_HALVES:
        upper = (t & half) != 0
        seg_lo = t - (t % half)
        seg_hi = t | (half - 1)
        mats.append(np.where(upper, (j >= seg_lo) & (j <= t), (j > t) & (j <= seg_hi)))
    dmat = np.concatenate(mats, axis=0).astype(np.float32)
    s = np.arange(4 * c)[None, :] % c
    x = t ^ s
    hb = np.floor(np.log2(np.maximum(x, 1))).astype(np.int32)
    lv = np.where(s > t, -1, np.where(s == t, 6, hb)).astype(np.int32)
    hm = (np.arange(4 * c)[:, None] // c == np.arange(4 * c)[None, :] // c).astype(np.float32)
    return dmat, lv, hm


def _hgrn_kernel(zq_ref, zf_ref, zi_ref, zg_ref, lb_ref, ng_ref, dmat_ref, lv_ref, hm_ref,
                 o_ref, st_ref):
    c = HGRN_CHUNK
    w = zq_ref.shape[2]
    nchunks = zq_ref.shape[1] // c

    lb = lb_ref[...]
    lbm = jnp.maximum(lb, LB_FLOOR)
    one_m_lb = 1.0 - lb
    k_off = lb - lbm
    ng = ng_ref[...]
    row = lax.broadcasted_iota(jnp.int32, (c, w), 0)

    def chunk(ci, st):
        r0 = pl.multiple_of(ci * c, c)
        zq = zq_ref[0, pl.ds(r0, c), :]
        zf = zf_ref[0, pl.ds(r0, c), :]
        v = zi_ref[0, pl.ds(r0, c), :]
        zg = zg_ref[0, pl.ds(r0, c), :]
        hm = hm_ref[...]
        hmb = hm.astype(BF16)
        lv = lv_ref[...]

        q = zq * _sigmoid(zq)
        logf = jnp.log(lbm + one_m_lb * _sigmoid(zf))
        k = one_m_lb * _sigmoid(-zf) + k_off

        dmat = dmat_ref[...]
        hi, mid, lo = _split3(logf)
        e_all = jnp.exp(_dot(dmat, hi) + _dot(dmat, mid) + _dot(dmat, lo))
        e_cum = e_all[0:c]
        e_rest = e_all[c:2 * c]

        p = jnp.zeros((c, 4 * c), F32)
        for li, half in enumerate(_LEVEL_HALVES):
            e_l = e_all[(2 + li) * c:(3 + li) * c]
            upper = (row & half) != 0
            q_l = jnp.where(upper, q * e_l, 0.0).astype(BF16)
            k_l = jnp.where(upper, 0.0, k * e_l)
            r_l = (jnp.concatenate([k_l] * HGRN_HEADS, axis=0) * hm).astype(BF16)
            s_l = _dot_nt(q_l, r_l)
            p = jnp.where(lv == int(math.log2(half)), s_l, p)
        s_d = _dot((q * k).astype(BF16), hmb)
        p = jnp.where(lv == 6, s_d, p)

        vbd = (jnp.concatenate([v] * HGRN_HEADS, axis=0) * hm).astype(BF16)
        o = _dot(p.astype(BF16), vbd) + _dot_nt((q * e_cum).astype(BF16), st.astype(BF16))

        upd = _dot_tn(v.astype(BF16), (k * e_rest).astype(BF16))
        st_new = st * e_cum[c - 1:c, :] + upd * hm

        oo_hi, oo_lo = _split2(o * o)
        ms = (_dot(oo_hi, hmb) + _dot(oo_lo, hmb)) * (1.0 / (w // HGRN_HEADS))
        out = o * lax.rsqrt(ms + RMS_EPS) * ng * (zg * _sigmoid(zg))
        o_ref[0, pl.ds(r0, c), :] = out
        return st_new

    lax.fori_loop(0, nchunks, chunk, jnp.zeros((w, w), F32), unroll=2)


def _hgrn(zq, zf, zi, zg, lb, ng):
    bsz, lp, w = zq.shape
    dmat, lv, hm = _hgrn_constants()
    dmat = jnp.asarray(dmat, BF16)
    lv = jnp.asarray(lv)
    hm = jnp.asarray(hm, F32)
    seq = pl.BlockSpec((1, lp, w), lambda b: (b, 0, 0))
    const = lambda b: (0, 0)
    return pl.pallas_call(
        _hgrn_kernel,
        grid=(bsz,),
        in_specs=[seq, seq, seq, seq,
                  pl.BlockSpec(lb.shape, const), pl.BlockSpec(ng.shape, const),
                  pl.BlockSpec(dmat.shape, const), pl.BlockSpec(lv.shape, const),
                  pl.BlockSpec(hm.shape, const)],
        out_specs=seq,
        out_shape=jax.ShapeDtypeStruct((bsz, lp, w), F32),
        scratch_shapes=[pltpu.VMEM((w, w), F32)],
        compiler_params=_params(("parallel",)),
        name="hgrn2",
    )(zq, zf, zi, zg, lb, ng, dmat, lv, hm)


def _attn_kernel(q_ref, k1_ref, k2_ref, v_ref, lam_ref, g_ref, o_ref, m1_ref, l1_ref, acc1_ref,
                 m2_ref, l2_ref, acc2_ref, *, tq, tk, out_scale):
    state = ((m1_ref, l1_ref, acc1_ref), (m2_ref, l2_ref, acc2_ref))
    i = pl.program_id(1)
    lp = k1_ref.shape[1]
    q0 = i * tq
    nkv = (q0 + tq + tk - 1) // tk
    n_full = jnp.maximum(q0 // tk, 1)
    kloc = lax.broadcasted_iota(jnp.int32, (tq, tk), 1)
    k_minus_q = kloc - lax.broadcasted_iota(jnp.int32, (tq, tk), 0)
    nrep = tk // 128
    lam = lam_ref[...]
    g = g_ref[...]

    for hd in range(ATTN_HEADS):
        cs = slice(hd * 128, (hd + 1) * 128)
        qh = q_ref[0, :, cs]
        for m_ref, l_ref, acc_ref in state:
            m_ref[...] = jnp.full(m_ref.shape, MASK_VALUE, F32)
            l_ref[...] = jnp.zeros(l_ref.shape, F32)
            acc_ref[...] = jnp.zeros(acc_ref.shape, F32)

        def block(j, masked):
            k0 = pl.multiple_of(jnp.minimum(j * tk, lp - tk), 128)
            vb = v_ref[0, pl.ds(k0, tk), cs]
            if masked:
                valid = (k_minus_q <= q0 - k0) & (kloc >= jnp.maximum(PAD, j * tk) - k0)
            for kr, (m_ref, l_ref, acc_ref) in zip((k1_ref, k2_ref), state):
                s = _dot_nt(qh, kr[0, pl.ds(k0, tk), cs])
                if masked:
                    s = jnp.where(valid, s, MASK_VALUE)
                m_prev = m_ref[...]
                m_next = jnp.maximum(m_prev, jnp.max(s, axis=-1, keepdims=True))
                alpha = jnp.exp2(m_prev - m_next)
                p = jnp.exp2(s - jnp.concatenate([m_next] * nrep, axis=1))
                psum = p[:, 0:128]
                for r in range(1, nrep):
                    psum = psum + p[:, r * 128:(r + 1) * 128]
                m_ref[...] = m_next
                l_ref[...] = alpha * l_ref[...] + psum
                acc_ref[...] = alpha * acc_ref[...] + _dot(p.astype(BF16), vb)

        def run(masked):
            def body(j, carry):
                block(j, masked)
                return carry
            return body

        block(0, True)
        lax.fori_loop(1, n_full, run(False), 0)
        lax.fori_loop(n_full, nkv, run(True), 0)

        l1 = jnp.sum(l1_ref[...], axis=-1, keepdims=True)
        l2 = jnp.sum(l2_ref[...], axis=-1, keepdims=True)
        o = acc1_ref[...] / l1 - lam * (acc2_ref[...] / l2)
        ms = jnp.mean(o * o, axis=-1, keepdims=True)
        o_ref[0, :, cs] = (o * lax.rsqrt(ms + RMS_EPS) * g * out_scale).astype(o_ref.dtype)


def _attn(q, k1, k2, v, lam_row, g_row, out_scale):
    bsz, lp, w = q.shape
    tq = _row_tile(lp, 544)
    tk = 256
    qs = pl.BlockSpec((1, tq, w), lambda b, i: (b, i, 0))
    full = pl.BlockSpec((1, lp, w), lambda b, i: (b, 0, 0))
    row = pl.BlockSpec((1, 128), lambda b, i: (0, 0))
    return pl.pallas_call(
        functools.partial(_attn_kernel, tq=tq, tk=tk, out_scale=out_scale),
        grid=(bsz, lp // tq),
        in_specs=[qs, full, full, full, row, row],
        out_specs=qs,
        out_shape=jax.ShapeDtypeStruct((bsz, lp, w), BF16),
        scratch_shapes=[pltpu.VMEM((tq, 128), F32)] * 6,
        compiler_params=_params(("parallel", "arbitrary")),
        name="diff_attn",
    )(q, k1, k2, v, lam_row, g_row)


def _router_logits(hn, wr_hi_ref, wr_lo_ref, br_ref):
    x_hi, x_lo = _split2(hn)
    return (_dot(x_hi, wr_hi_ref[...]) + _dot(x_hi, wr_lo_ref[...]) + _dot(x_lo, wr_hi_ref[...])
            + br_ref[...])


def _router_group(logits):
    lane = lax.broadcasted_iota(jnp.int32, logits.shape, 1)
    is_g = lane < MOE_GROUPS
    glog = jnp.where(is_g, logits, MASK_VALUE)
    gmax = jnp.max(glog, axis=-1, keepdims=True)
    gsum = jnp.sum(jnp.where(is_g, jnp.exp(glog - gmax), 0.0), axis=-1, keepdims=True)
    gi = jnp.min(jnp.where(is_g & (glog == gmax), lane.astype(F32), 1e9), axis=-1, keepdims=True)
    return gi, 1.0 / gsum


def _router_gates(logits, gi, gp):
    lane = lax.broadcasted_iota(jnp.int32, logits.shape, 1)
    lanef = lane.astype(F32)
    big = 1e9
    e_idx = lane - MOE_GROUPS
    in_group = (e_idx >= 0) & (e_idx < MOE_EXPERTS) & ((e_idx >> 2) == gi)
    elog = jnp.where(in_group, logits, MASK_VALUE)
    v1 = jnp.max(elog, axis=-1, keepdims=True)
    i1 = jnp.min(jnp.where(in_group & (elog == v1), lanef, big), axis=-1, keepdims=True)
    rest = in_group & (lanef != i1)
    elog2 = jnp.where(rest, logits, MASK_VALUE)
    v2 = jnp.max(elog2, axis=-1, keepdims=True)
    i2 = jnp.min(jnp.where(rest & (elog2 == v2), lanef, big), axis=-1, keepdims=True)
    e21 = jnp.exp(v2 - v1)
    den = 1.0 / (1.0 + e21)
    return jnp.where(lanef == i1, gp * den, 0.0) + jnp.where(lanef == i2, gp * e21 * den, 0.0)


def _out_proj_kernel(h_ref, ys_ref, yr_ref, ya_ref, wo_ref, gn_ref, wr_hi_ref, wr_lo_ref, br_ref,
                     hout_ref, gi_ref, *, tl):
    acc = _dot(ys_ref[...].astype(BF16), wo_ref[0:256, :])
    acc += _dot(yr_ref[0].astype(BF16), wo_ref[256:512, :])
    acc += _dot(ya_ref[0], wo_ref[512:1024, :])
    h = _load_tokens(h_ref, tl) + acc
    rowpos = pl.program_id(1) * tl + lax.broadcasted_iota(jnp.int32, h.shape, 0)
    h = jnp.where(rowpos >= PAD, h, 0.0)
    _store_tokens(hout_ref, h)
    ms = jnp.mean(h * h, axis=-1, keepdims=True)
    hn = h * lax.rsqrt(ms + RMS_EPS) * gn_ref[...]
    gi, _ = _router_group(_router_logits(hn, wr_hi_ref, wr_lo_ref, br_ref))
    gi_ref[0] = gi.astype(jnp.int32)


def _out_proj(h_slab, bsz, lp, ys2d, yr, ya, wo_bf, gn, wr_hi, wr_lo, br):
    d = SLAB * 128
    tl = _row_tile(lp, 544)
    nl = lp // tl
    row = lambda b, i: (b, i, 0)
    const = lambda b, i: (0, 0)
    slab = pl.BlockSpec((tl * SLAB, 128), lambda b, i: (b * nl + i, 0))
    return pl.pallas_call(
        functools.partial(_out_proj_kernel, tl=tl),
        grid=(bsz, nl),
        in_specs=[slab,
                  pl.BlockSpec((tl, 256), lambda b, i: (i, b)),
                  pl.BlockSpec((1, tl, 256), row),
                  pl.BlockSpec((1, tl, 512), row),
                  pl.BlockSpec(wo_bf.shape, const),
                  pl.BlockSpec(gn.shape, const),
                  pl.BlockSpec(wr_hi.shape, const),
                  pl.BlockSpec(wr_lo.shape, const),
                  pl.BlockSpec(br.shape, const)],
        out_specs=(slab, pl.BlockSpec((1, tl, 1), row)),
        out_shape=(jax.ShapeDtypeStruct((bsz * lp * SLAB, 128), F32),
                   jax.ShapeDtypeStruct((bsz, lp, 1), jnp.int32)),
        compiler_params=_params(("parallel", "parallel")),
        name="out_proj_router",
    )(h_slab, ys2d, yr, ya, wo_bf, gn, wr_hi, wr_lo, br)


def _moe_kernel(grp_ref, used_ref, src_ref, srcn_ref, dst_ref, h_hbm, gn_ref, wr_hi_ref, wr_lo_ref, br_ref,
                wg_ref, wu_ref, wd_ref, out_hbm, xbuf, ybuf, gsem, ssem, *, tm):
    p = pl.program_id(0)
    used = used_ref[0]
    g = grp_ref[p]
    slot = p % 2

    def for_rows(start_row):
        def body(i, carry):
            for u in range(ROW_UNROLL):
                start_row(i * ROW_UNROLL + u, u % 2)
            return carry
        lax.fori_loop(0, tm // ROW_UNROLL, body, 0)

    def gather(idx_ref, into):
        def start_row(r, prio):
            rows = pl.ds(pl.multiple_of((into * tm + r) * SLAB, SLAB), SLAB)
            pltpu.make_async_copy(h_hbm.at[idx_ref[0, 0, r]], xbuf.at[rows, :],
                                  gsem.at[into]).start(priority=prio)
        for_rows(start_row)

    def scatter(dst_of_row):
        def start_row(r, prio):
            rows = pl.ds(pl.multiple_of(r * SLAB, SLAB), SLAB)
            pltpu.make_async_copy(ybuf.at[rows, :], out_hbm.at[dst_of_row(r)],
                                  ssem.at[0]).start(priority=prio)
        for_rows(start_row)

    def wait_tile(sem):
        pltpu.make_async_copy(h_hbm.at[pl.ds(0, tm)], out_hbm.at[pl.ds(0, tm)], sem).wait()

    @pl.when(p == 0)
    def _():
        gather(src_ref, 0)
        spare = out_hbm.shape[0] - tm
        ybuf[...] = jnp.zeros_like(ybuf)
        scatter(lambda r: spare + r)
        wait_tile(ssem.at[0])

    @pl.when(p < used)
    def _():
        wait_tile(gsem.at[slot])

        for s in range(2):
            @pl.when((p + 1 < used) & (slot == s))
            def _():
                gather(srcn_ref, 1 - s)

        h = _load_tokens(xbuf, tm, first=slot * tm)
        ms = jnp.mean(h * h, axis=-1, keepdims=True)
        hn = h * lax.rsqrt(ms + RMS_EPS) * gn_ref[...]
        logits = _router_logits(hn, wr_hi_ref, wr_lo_ref, br_ref)
        lane = lax.broadcasted_iota(jnp.int32, logits.shape, 1)
        is_g = lane < MOE_GROUPS
        glog = jnp.where(is_g, logits, MASK_VALUE)
        gmax = jnp.max(glog, axis=-1, keepdims=True)
        gsum = jnp.sum(jnp.where(is_g, jnp.exp(glog - gmax), 0.0), axis=-1, keepdims=True)
        lg = jnp.sum(jnp.where(lane == g, logits, 0.0), axis=-1, keepdims=True)
        gates = _router_gates(logits, g, jnp.exp(lg - gmax) / gsum)

        x = hn.astype(BF16)
        y = h
        for e in range(MOE_EXPERTS_PER_GROUP):
            ge = jnp.sum(jnp.where(lane == MOE_GROUPS + MOE_EXPERTS_PER_GROUP * g + e, gates, 0.0),
                         axis=-1, keepdims=True)
            a = _dot(x, wg_ref[e])
            b = _dot(x, wu_ref[e])
            he = (a * _sigmoid(a)) * b * ge
            y = y + _dot(he.astype(BF16), wd_ref[e])

        @pl.when(p > 0)
        def _():
            wait_tile(ssem.at[0])

        _store_tokens(ybuf, y)
        scatter(lambda r: dst_ref[0, 0, r])

        @pl.when(p + 1 == used)
        def _():
            wait_tile(ssem.at[0])


def _moe_routed(h_slab, gi, gn, wr_hi, wr_lo, br, wg, wu, wd, tm=512):
    t = gi.shape[0]
    ne, d, ff = wg.shape
    tm = _row_tile(t, tm)
    nt = t // tm + MOE_GROUPS

    order = jnp.argsort(gi, stable=True).astype(jnp.int32)
    counts = jnp.sum((gi[:, None] == jnp.arange(MOE_GROUPS, dtype=jnp.int32)[None, :]).astype(jnp.int32),
                     axis=0)
    starts = jnp.cumsum(counts) - counts
    tiles_g = (counts + tm - 1) // tm
    tile_end = jnp.cumsum(tiles_g)
    tile_start = tile_end - tiles_g
    pidx = jnp.arange(nt, dtype=jnp.int32)
    grp = jnp.minimum(jnp.sum((pidx[:, None] >= tile_end[None, :]).astype(jnp.int32), axis=1),
                      MOE_GROUPS - 1)
    j = pidx - tile_start[grp]
    nval = jnp.where(pidx < tile_end[-1], jnp.clip(counts[grp] - j * tm, 0, tm), 0)
    lane = jnp.arange(tm, dtype=jnp.int32)[None, :]
    rows = starts[grp][:, None] + j[:, None] * tm + lane
    src = order[jnp.clip(rows, 0, t - 1)]
    dst = jnp.where(lane < nval[:, None], src, t + lane)
    src = src.reshape(nt, 1, tm)
    dst = dst.reshape(nt, 1, tm).astype(jnp.int32)
    used = tile_end[-1:].astype(jnp.int32)

    const2 = lambda p, grp, used: (0, 0)
    wspec = lambda shape: pl.BlockSpec((MOE_EXPERTS_PER_GROUP,) + shape, lambda p, grp, used: (grp[p], 0, 0))
    idx_spec = lambda shift: pl.BlockSpec(
        (1, 1, tm), lambda p, grp, used: (jnp.minimum(p + shift, nt - 1), 0, 0), memory_space=pltpu.SMEM)
    grid_spec = pltpu.PrefetchScalarGridSpec(
        num_scalar_prefetch=2,
        grid=(nt,),
        in_specs=[idx_spec(0), idx_spec(1), idx_spec(0),
                  pl.BlockSpec(memory_space=pl.ANY),
                  pl.BlockSpec(gn.shape, const2),
                  pl.BlockSpec(wr_hi.shape, const2),
                  pl.BlockSpec(wr_lo.shape, const2),
                  pl.BlockSpec(br.shape, const2),
                  wspec((d, ff)), wspec((d, ff)), wspec((ff, d))],
        out_specs=pl.BlockSpec(memory_space=pl.ANY),
        scratch_shapes=[pltpu.VMEM((2 * tm * SLAB, 128), F32), pltpu.VMEM((tm * SLAB, 128), F32),
                        pltpu.SemaphoreType.DMA((2,)), pltpu.SemaphoreType.DMA((1,))],
    )
    n_in = h_slab.shape[0] // SLAB
    out = pl.pallas_call(
        functools.partial(_moe_kernel, tm=tm),
        grid_spec=grid_spec,
        out_shape=jax.ShapeDtypeStruct((t + tm, SLAB, 128), F32),
        compiler_params=_params(("arbitrary",)),
        name="moe_experts",
    )(grp, used, src, src, dst, h_slab.reshape(n_in, SLAB, 128), gn, wr_hi, wr_lo, br, wg, wu, wd)
    return out.reshape((t + tm) * SLAB, 128)


def _final_norm_kernel(h_ref, g_ref, o_ref, *, skip, chunk):
    g = g_ref[...]
    for c in range(o_ref.shape[1] // chunk):
        x = _load_tokens(h_ref, chunk, first=skip + c * chunk)
        ms = jnp.mean(x * x, axis=-1, keepdims=True)
        o_ref[0, c * chunk:(c + 1) * chunk, :] = x * lax.rsqrt(ms + RMS_EPS) * g


def _final_norm(h_slab, bsz, lp, g, seq):
    d = SLAB * 128
    chunk = _row_tile(seq, 512)
    return pl.pallas_call(
        functools.partial(_final_norm_kernel, skip=lp - seq, chunk=chunk),
        grid=(bsz,),
        in_specs=[pl.BlockSpec((lp * SLAB, 128), lambda b: (b, 0)),
                  pl.BlockSpec((1, d), lambda b: (0, 0))],
        out_specs=pl.BlockSpec((1, seq, d), lambda b: (b, 0, 0)),
        out_shape=jax.ShapeDtypeStruct((bsz, seq, d), F32),
        compiler_params=_params(("parallel",)),
        name="final_norm",
    )(h_slab, g.reshape(1, d))


def _s5_tables(lam_re, lam_im, log_dt, b_re, b_im, c_re, c_im):
    ng, ns = lam_re.shape
    lr = lam_re.astype(F32)
    li = lam_im.astype(F32)
    dt = jnp.exp(log_dt.astype(F32))[:, None]
    mag = jnp.exp(lr * dt)
    abar_r = mag * jnp.cos(li * dt)
    abar_i = mag * jnp.sin(li * dt)
    den = lr * lr + li * li
    zr = abar_r - 1.0
    zi = abar_i
    fr = (zr * lr + zi * li) / den
    fi = (zi * lr - zr * li) / den
    br = b_re.astype(F32)
    bi = b_im.astype(F32)
    bb_r = fr[..., None] * br - fi[..., None] * bi
    bb_i = fr[..., None] * bi + fi[..., None] * br
    eye = jnp.eye(ng, dtype=F32)
    nch = b_re.shape[2]
    to_b = lambda m: jnp.einsum('gnc,gh->gchn', m, eye).reshape(ng * nch, ng * ns)
    bbar = jnp.concatenate([to_b(bb_r), to_b(bb_i)], axis=1).astype(BF16)
    to_c = lambda m: jnp.einsum('gcn,gh->gnhc', m.astype(F32), eye).reshape(ng * ns, ng * nch)
    cmat = jnp.concatenate([to_c(c_re), -to_c(c_im)], axis=0).astype(BF16)
    a_row = jnp.concatenate([abar_r.reshape(1, -1), abar_i.reshape(1, -1)], axis=1)
    a8 = jnp.broadcast_to(a_row, (8, a_row.shape[1]))
    return bbar, a8, cmat


def _rope_tables(lp):
    half = 32
    inv_freq = 1.0 / (ROPE_THETA ** (jnp.arange(0, 2 * half, 2, dtype=F32) / (2 * half)))
    pos = jnp.arange(lp, dtype=F32) - float(PAD)
    ang = pos[:, None] * inv_freq[None, :]
    ang = jnp.concatenate([ang, ang, ang, ang], axis=-1)
    cos = jnp.cos(ang)
    sin = jnp.sin(ang)
    first = (jnp.arange(128) % 64) < half
    sina = jnp.where(first[None, :], -sin, 0.0)
    sinb = jnp.where(first[None, :], 0.0, sin)
    return cos, sina, sinb


def kernel(x, meta_tokens, norm_mix_g, w_in, s5_lambda_re, s5_lambda_im, s5_log_dt, s5_b_re, s5_b_im, s5_c_re, s5_c_im, s5_d, s5_w_glu, hgrn_lower_bounds, hgrn_norm_g, diff_lambda_q1, diff_lambda_k1, diff_lambda_q2, diff_lambda_k2, diff_subln_g, w_out, norm_ffn_g, moe_w_group, moe_b_group, moe_w_expert, moe_b_expert, moe_w_gate, moe_w_up, moe_w_down, final_norm_g):
    bsz, seq, d = x.shape
    depth = w_in.shape[0]
    lp = PAD + N_META + seq
    assert lp % 128 == 0 and bsz % 8 == 0

    assert d == SLAB * 128 and meta_tokens.shape[0] == N_META
    h = _embed(x.astype(F32), meta_tokens.astype(F32))

    cos, sina, sinb = _rope_tables(lp)
    lb_w = jax.nn.softmax(hgrn_lower_bounds.astype(F32), axis=0)
    lower_bounds = jnp.cumsum(lb_w, axis=0) - lb_w[0:1]

    for layer in range(depth):
        lam_init = 0.8 - 0.6 * math.exp(-0.3 * layer)
        u2d, hq, hf, hi, hg, aq, ak1, ak2, av = _in_proj(
            h, bsz, lp, norm_mix_g[layer].reshape(1, d).astype(F32), w_in[layer].astype(BF16), cos, sina, sinb)

        bbar, a8, cmat = _s5_tables(s5_lambda_re[layer], s5_lambda_im[layer], s5_log_dt[layer],
                                    s5_b_re[layer], s5_b_im[layer], s5_c_re[layer], s5_c_im[layer])
        y_ssm = _s5(u2d.reshape(lp, bsz, 256), bbar, a8, cmat,
                    s5_d[layer].reshape(1, -1).astype(F32), s5_w_glu[layer].astype(BF16))

        ng = jnp.tile(hgrn_norm_g[layer].astype(F32), HGRN_HEADS).reshape(1, -1)
        y_rec = _hgrn(hq, hf, hi, hg, lower_bounds[layer].reshape(1, -1), ng)

        lam = (jnp.exp(jnp.sum(diff_lambda_q1[layer].astype(F32) * diff_lambda_k1[layer].astype(F32)))
               - jnp.exp(jnp.sum(diff_lambda_q2[layer].astype(F32) * diff_lambda_k2[layer].astype(F32)))
               + lam_init)
        y_att = _attn(aq, ak1, ak2, av, jnp.full((1, 128), lam, F32),
                      diff_subln_g[layer].reshape(1, -1).astype(F32), 1.0 - lam_init)

        wr = jnp.concatenate([moe_w_group[layer].astype(F32), moe_w_expert[layer].astype(F32)], axis=1)
        wr = jnp.pad(wr, ((0, 0), (0, ROUTER_LANES - wr.shape[1])))
        wr_hi = wr.astype(BF16)
        wr_lo = (wr - wr_hi.astype(F32)).astype(BF16)
        br = jnp.concatenate([moe_b_group[layer].astype(F32), moe_b_expert[layer].astype(F32)])
        br = jnp.pad(br, (0, ROUTER_LANES - br.shape[0])).reshape(1, -1)
        gn = norm_ffn_g[layer].reshape(1, d).astype(F32)
        h, gi = _out_proj(h, bsz, lp, y_ssm.reshape(lp, bsz * 256), y_rec, y_att,
                          w_out[layer].astype(BF16), gn, wr_hi, wr_lo, br)
        h = _moe_routed(h, gi.reshape(bsz * lp), gn, wr_hi, wr_lo, br, moe_w_gate[layer].astype(BF16),
                        moe_w_up[layer].astype(BF16), moe_w_down[layer].astype(BF16))

    return _final_norm(h, bsz, lp, final_norm_g.astype(F32), seq)
```

```python
import functools
import math

import numpy as np
import jax
import jax.numpy as jnp
from jax import lax
from jax.experimental import pallas as pl
from jax.experimental.pallas import tpu as pltpu

F32 = jnp.float32
BF16 = jnp.bfloat16

N_META = 16
PAD = 112
RMS_EPS = 1e-6
MASK_VALUE = -1e30
LB_FLOOR = 1e-30
ROPE_THETA = 10000.0

S5_GROUP_CH = 16
S5_STATE = 64
HGRN_HEADS = 4
HGRN_CHUNK = 64
ATTN_HEADS = 4
ATTN_QK_DIM = 64
Q_SCALE = ATTN_QK_DIM ** -0.5 * math.log2(math.e)
MOE_GROUPS = 4
MOE_EXPERTS_PER_GROUP = 4
MOE_EXPERTS = MOE_GROUPS * MOE_EXPERTS_PER_GROUP
ROUTER_LANES = 128
ROW_UNROLL = 8

VMEM_LIMIT = 56 * 1024 * 1024


def _params(sem, vmem=VMEM_LIMIT):
    return pltpu.CompilerParams(dimension_semantics=sem, vmem_limit_bytes=vmem)


def _row_tile(n, cap, mult=16):
    best = None
    for t in range(mult, min(n, cap) + 1, mult):
        if n % t == 0:
            best = t
    assert best is not None, (n, cap)
    return best


def _sigmoid(x):
    return 1.0 / (1.0 + jnp.exp(-x))


def _dot(a, b):
    return jnp.dot(a, b, preferred_element_type=F32)


def _dot_nt(a, b):
    return lax.dot_general(a, b, (((1,), (1,)), ((), ())), preferred_element_type=F32)


def _dot_tn(a, b):
    return lax.dot_general(a, b, (((0,), (0,)), ((), ())), preferred_element_type=F32)


SLAB = 8


def _load_tokens(ref, n, first=0):
    return jnp.concatenate([ref[pl.ds(first * SLAB + s, n, stride=SLAB), :] for s in range(SLAB)], axis=1)


def _store_tokens(ref, val, first=0):
    n = val.shape[0]
    for s in range(SLAB):
        ref[pl.ds(first * SLAB + s, n, stride=SLAB), :] = val[:, s * 128:(s + 1) * 128]


def _embed_kernel(x_ref, meta_ref, o_ref, *, chunk):
    seq, d = x_ref.shape[1], x_ref.shape[2]
    _store_tokens(o_ref, jnp.zeros((PAD, d), F32))
    _store_tokens(o_ref, meta_ref[...], first=PAD)
    for c in range(seq // chunk):
        _store_tokens(o_ref, x_ref[0, c * chunk:(c + 1) * chunk, :], first=PAD + N_META + c * chunk)


def _embed(x, meta):
    bsz, seq, d = x.shape
    lp = PAD + N_META + seq
    return pl.pallas_call(
        functools.partial(_embed_kernel, chunk=_row_tile(seq, 512)),
        grid=(bsz,),
        in_specs=[pl.BlockSpec((1, seq, d), lambda b: (b, 0, 0)),
                  pl.BlockSpec(meta.shape, lambda b: (0, 0))],
        out_specs=pl.BlockSpec((lp * SLAB, 128), lambda b: (b, 0)),
        out_shape=jax.ShapeDtypeStruct((bsz * lp * SLAB, 128), F32),
        compiler_params=_params(("parallel",)),
        name="embed",
    )(x, meta)


def _split2(x):
    hi = x.astype(BF16)
    lo = (x - hi.astype(F32)).astype(BF16)
    return hi, lo


def _split3(x):
    hi = x.astype(BF16)
    r = x - hi.astype(F32)
    mid = r.astype(BF16)
    lo = (r - mid.astype(F32)).astype(BF16)
    return hi, mid, lo


def _in_proj_kernel(h_ref, g_ref, w_ref, cos_ref, sina_ref, sinb_ref,
                    u_ref, hq_ref, hf_ref, hi_ref, hg_ref, q_ref, k1_ref, k2_ref, v_ref):
    x = _load_tokens(h_ref, u_ref.shape[0])
    ms = jnp.mean(x * x, axis=-1, keepdims=True)
    xn = (x * lax.rsqrt(ms + RMS_EPS) * g_ref[...]).astype(BF16)

    def sec(lo, hi):
        return _dot(xn, w_ref[:, lo:hi])

    u_ref[...] = sec(0, 256)
    hq_ref[0] = sec(256, 512)
    hf_ref[0] = sec(512, 768)
    hi_ref[0] = sec(768, 1024)
    hg_ref[0] = sec(1024, 1280)

    cos = cos_ref[...]
    sina = sina_ref[...]
    sinb = sinb_ref[...]
    lane = lax.broadcasted_iota(jnp.int32, cos.shape, 1)
    first = lane < 64

    def rope(t):
        return t * cos + pltpu.roll(t, 96, 1) * sina + pltpu.roll(t, 32, 1) * sinb

    for hd in range(ATTN_HEADS):
        c0 = 1280 + hd * 128
        q = rope(sec(c0, c0 + 128)) * Q_SCALE
        q_ref[0, :, hd * 128:(hd + 1) * 128] = q.astype(BF16)
        c0 = 1792 + hd * 128
        k = rope(sec(c0, c0 + 128))
        k1_ref[0, :, hd * 128:(hd + 1) * 128] = jnp.where(first, k, 0.0).astype(BF16)
        k2_ref[0, :, hd * 128:(hd + 1) * 128] = jnp.where(first, 0.0, k).astype(BF16)
    v_ref[0] = sec(2304, 2816).astype(BF16)


def _in_proj(h_slab, bsz, lp, g, w_bf, cos, sina, sinb):
    d = SLAB * 128
    tl = _row_tile(lp, 544)
    nl = lp // tl
    row = lambda b, i: (b, i, 0)
    tab = pl.BlockSpec((tl, 128), lambda b, i: (i, 0))
    out_shape = (
        jax.ShapeDtypeStruct((lp, bsz * 256), F32),
        jax.ShapeDtypeStruct((bsz, lp, 256), F32),
        jax.ShapeDtypeStruct((bsz, lp, 256), F32),
        jax.ShapeDtypeStruct((bsz, lp, 256), F32),
        jax.ShapeDtypeStruct((bsz, lp, 256), F32),
        jax.ShapeDtypeStruct((bsz, lp, 512), BF16),
        jax.ShapeDtypeStruct((bsz, lp, 512), BF16),
        jax.ShapeDtypeStruct((bsz, lp, 512), BF16),
        jax.ShapeDtypeStruct((bsz, lp, 512), BF16),
    )
    s256 = pl.BlockSpec((1, tl, 256), row)
    s512 = pl.BlockSpec((1, tl, 512), row)
    return pl.pallas_call(
        _in_proj_kernel,
        grid=(bsz, nl),
        in_specs=[pl.BlockSpec((tl * SLAB, 128), lambda b, i: (b * nl + i, 0)),
                  pl.BlockSpec((1, d), lambda b, i: (0, 0)),
                  pl.BlockSpec(w_bf.shape, lambda b, i: (0, 0)),
                  tab, tab, tab],
        out_specs=(pl.BlockSpec((tl, 256), lambda b, i: (i, b)), s256, s256, s256, s256,
                   s512, s512, s512, s512),
        out_shape=out_shape,
        compiler_params=_params(("parallel", "parallel")),
        name="in_proj",
    )(h_slab, g, w_bf, cos, sina, sinb)


def _s5_kernel(u_ref, bbar_ref, a_ref, cmat_ref, d_ref, wglu_ref, y_ref, xs_ref, st_ref):
    tt = u_ref.shape[0]
    nst = a_ref.shape[1] // 2

    @pl.when(pl.program_id(1) == 0)
    def _():
        st_ref[...] = jnp.zeros_like(st_ref)

    u2 = u_ref[...].reshape(tt * 8, u_ref.shape[2])
    xs_ref[...] = _dot(u2.astype(BF16), bbar_ref[...])

    ar = a_ref[:, :nst]
    ai = a_ref[:, nst:]

    def step(t, carry):
        sr, si = carry
        r0 = pl.multiple_of(t * 8, 8)
        xr = xs_ref[pl.ds(r0, 8), :nst]
        xi = xs_ref[pl.ds(r0, 8), nst:]
        nr = ar * sr - ai * si + xr
        ni = ar * si + ai * sr + xi
        xs_ref[pl.ds(r0, 8), :nst] = nr
        xs_ref[pl.ds(r0, 8), nst:] = ni
        return nr, ni

    sr, si = lax.fori_loop(0, tt, step, (st_ref[:, :nst], st_ref[:, nst:]))
    st_ref[:, :nst] = sr
    st_ref[:, nst:] = si

    y = _dot(xs_ref[...].astype(BF16), cmat_ref[...]) + d_ref[...] * u2
    g = 0.5 * y * (1.0 + jnp.tanh(0.7978845608028654 * (y + 0.044715 * (y * y * y))))
    gl = _dot(g.astype(BF16), wglu_ref[...])
    w = gl.shape[1] // 2
    out = gl[:, :w] * _sigmoid(gl[:, w:])
    y_ref[...] = out.reshape(y_ref.shape)


def _s5(u3, bbar, a8, cmat, dskip, wglu):
    lp, bsz, w = u3.shape
    tt = 128
    nst2 = a8.shape[1]
    const = lambda bg, i: (0, 0)
    return pl.pallas_call(
        _s5_kernel,
        grid=(bsz // 8, lp // tt),
        in_specs=[pl.BlockSpec((tt, 8, w), lambda bg, i: (i, bg, 0)),
                  pl.BlockSpec(bbar.shape, const),
                  pl.BlockSpec(a8.shape, const),
                  pl.BlockSpec(cmat.shape, const),
                  pl.BlockSpec(dskip.shape, const),
                  pl.BlockSpec(wglu.shape, const)],
        out_specs=pl.BlockSpec((tt, 8, w), lambda bg, i: (i, bg, 0)),
        out_shape=jax.ShapeDtypeStruct((lp, bsz, w), F32),
        scratch_shapes=[pltpu.VMEM((tt * 8, nst2), F32), pltpu.VMEM((8, nst2), F32)],
        compiler_params=_params(("parallel", "arbitrary")),
        name="s5_scan",
    )(u3, bbar, a8, cmat, dskip, wglu)


_LEVEL_HALVES = (32, 16, 8, 4, 2, 1)


def _hgrn_constants():
    c = HGRN_CHUNK
    t = np.arange(c)[:, None]
    j = np.arange(c)[None, :]
    mats = [(j <= t), (j > t)]
    for half in _LEVEL_HALVES:
        upper = (t & half) != 0
        seg_lo = t - (t % half)
        seg_hi = t | (half - 1)
        mats.append(np.where(upper, (j >= seg_lo) & (j <= t), (j > t) & (j <= seg_hi)))
    dmat = np.concatenate(mats, axis=0).astype(np.float32)
    s = np.arange(4 * c)[None, :] % c
    x = t ^ s
    hb = np.floor(np.log2(np.maximum(x, 1))).astype(np.int32)
    lv = np.where(s > t, -1, np.where(s == t, 6, hb)).astype(np.int32)
    hm = (np.arange(4 * c)[:, None] // c == np.arange(4 * c)[None, :] // c).astype(np.float32)
    return dmat, lv, hm


def _hgrn_kernel(zq_ref, zf_ref, zi_ref, zg_ref, lb_ref, ng_ref, dmat_ref, lv_ref, hm_ref,
                 o_ref, st_ref):
    c = HGRN_CHUNK
    w = zq_ref.shape[2]
    nchunks = zq_ref.shape[1] // c

    lb = lb_ref[...]
    lbm = jnp.maximum(lb, LB_FLOOR)
    one_m_lb = 1.0 - lb
    k_off = lb - lbm
    ng = ng_ref[...]
    row = lax.broadcasted_iota(jnp.int32, (c, w), 0)

    def chunk(ci, st):
        r0 = pl.multiple_of(ci * c, c)
        zq = zq_ref[0, pl.ds(r0, c), :]
        zf = zf_ref[0, pl.ds(r0, c), :]
        v = zi_ref[0, pl.ds(r0, c), :]
        zg = zg_ref[0, pl.ds(r0, c), :]
        hm = hm_ref[...]
        hmb = hm.astype(BF16)
        lv = lv_ref[...]

        q = zq * _sigmoid(zq)
        logf = jnp.log(lbm + one_m_lb * _sigmoid(zf))
        k = one_m_lb * _sigmoid(-zf) + k_off

        dmat = dmat_ref[...]
        hi, mid, lo = _split3(logf)
        e_all = jnp.exp(_dot(dmat, hi) + _dot(dmat, mid) + _dot(dmat, lo))
        e_cum = e_all[0:c]
        e_rest = e_all[c:2 * c]

        p = jnp.zeros((c, 4 * c), F32)
        for li, half in enumerate(_LEVEL_HALVES):
            e_l = e_all[(2 + li) * c:(3 + li) * c]
            upper = (row & half) != 0
            q_l = jnp.where(upper, q * e_l, 0.0).astype(BF16)
            k_l = jnp.where(upper, 0.0, k * e_l).astype(BF16)
            r_l = jnp.concatenate([k_l] * HGRN_HEADS, axis=0) * hmb
            s_l = _dot_nt(q_l, r_l)
            p = jnp.where(lv == int(math.log2(half)), s_l, p)
        s_d = _dot((q * k).astype(BF16), hmb)
        p = jnp.where(lv == 6, s_d, p)

        vb = v.astype(BF16)
        vbd = jnp.concatenate([vb] * HGRN_HEADS, axis=0) * hmb
        o = _dot(p.astype(BF16), vbd) + _dot_nt((q * e_cum).astype(BF16), st.astype(BF16))

        upd = _dot_tn(vb, (k * e_rest).astype(BF16))
        st_new = st * e_cum[c - 1:c, :] + upd * hm

        oo_hi, oo_lo = _split2(o * o)
        ms = (_dot(oo_hi, hmb) + _dot(oo_lo, hmb)) * (1.0 / (w // HGRN_HEADS))
        out = o * lax.rsqrt(ms + RMS_EPS) * ng * (zg * _sigmoid(zg))
        o_ref[0, pl.ds(r0, c), :] = out
        return st_new

    lax.fori_loop(0, nchunks, chunk, jnp.zeros((w, w), F32), unroll=2)


def _hgrn(zq, zf, zi, zg, lb, ng):
    bsz, lp, w = zq.shape
    dmat, lv, hm = _hgrn_constants()
    dmat = jnp.asarray(dmat, BF16)
    lv = jnp.asarray(lv)
    hm = jnp.asarray(hm, F32)
    seq = pl.BlockSpec((1, lp, w), lambda b: (b, 0, 0))
    const = lambda b: (0, 0)
    return pl.pallas_call(
        _hgrn_kernel,
        grid=(bsz,),
        in_specs=[seq, seq, seq, seq,
                  pl.BlockSpec(lb.shape, const), pl.BlockSpec(ng.shape, const),
                  pl.BlockSpec(dmat.shape, const), pl.BlockSpec(lv.shape, const),
                  pl.BlockSpec(hm.shape, const)],
        out_specs=seq,
        out_shape=jax.ShapeDtypeStruct((bsz, lp, w), F32),
        scratch_shapes=[pltpu.VMEM((w, w), F32)],
        compiler_params=_params(("parallel",)),
        name="hgrn2",
    )(zq, zf, zi, zg, lb, ng, dmat, lv, hm)


def _attn_kernel(q_ref, k1_ref, k2_ref, v_ref, lam_ref, g_ref, o_ref, m1_ref, l1_ref, acc1_ref,
                 m2_ref, l2_ref, acc2_ref, sa1_ref, sa2_ref, sb1_ref, sb2_ref, *, tq, tk, out_scale):
    state = ((m1_ref, l1_ref, acc1_ref), (m2_ref, l2_ref, acc2_ref))
    s_a = (sa1_ref, sa2_ref)
    s_b = (sb1_ref, sb2_ref)
    i = pl.program_id(1)
    lp = k1_ref.shape[1]
    q0 = i * tq
    nkv = (q0 + tq + tk - 1) // tk
    kloc = lax.broadcasted_iota(jnp.int32, (tq, tk), 1)
    k_minus_q = kloc - lax.broadcasted_iota(jnp.int32, (tq, tk), 0)
    nrep = tk // 128
    lam = lam_ref[...]
    g = g_ref[...]

    for hd in range(ATTN_HEADS):
        cs = slice(hd * 128, (hd + 1) * 128)
        qh = q_ref[0, :, cs]
        for m_ref, l_ref, acc_ref in state:
            m_ref[...] = jnp.full(m_ref.shape, MASK_VALUE, F32)
            l_ref[...] = jnp.zeros(l_ref.shape, F32)
            acc_ref[...] = jnp.zeros(acc_ref.shape, F32)

        def key_start(j):
            return pl.multiple_of(jnp.minimum(j * tk, lp - tk), 128)

        def scores(j, into):
            k0 = key_start(j)
            for kr, s_ref in zip((k1_ref, k2_ref), into):
                s_ref[...] = _dot_nt(qh, kr[0, pl.ds(k0, tk), cs])

        def block(j, cur, nxt):
            if nxt is not None:
                scores(j + 1, nxt)
            k0 = key_start(j)
            vb = v_ref[0, pl.ds(k0, tk), cs]
            valid = (k_minus_q <= q0 - k0) & (kloc >= jnp.maximum(PAD, j * tk) - k0)
            for s_ref, (m_ref, l_ref, acc_ref) in zip(cur, state):
                s = jnp.where(valid, s_ref[...], MASK_VALUE)
                m_prev = m_ref[...]
                m_next = jnp.maximum(m_prev, jnp.max(s, axis=-1, keepdims=True))
                alpha = jnp.exp2(m_prev - m_next)
                p = jnp.exp2(s - jnp.concatenate([m_next] * nrep, axis=1))
                psum = p[:, 0:128]
                for r in range(1, nrep):
                    psum = psum + p[:, r * 128:(r + 1) * 128]
                m_ref[...] = m_next
                l_ref[...] = alpha * l_ref[...] + psum
                acc_ref[...] = alpha * acc_ref[...] + _dot(p.astype(BF16), vb)

        def pair(t, carry):
            block(2 * t, s_a, s_b)
            block(2 * t + 1, s_b, s_a)
            return carry

        scores(0, s_a)
        lax.fori_loop(0, nkv // 2, pair, 0)

        @pl.when(nkv % 2 == 1)
        def _():
            block(nkv - 1, s_a, None)

        l1 = jnp.sum(l1_ref[...], axis=-1, keepdims=True)
        l2 = jnp.sum(l2_ref[...], axis=-1, keepdims=True)
        o = acc1_ref[...] / l1 - lam * (acc2_ref[...] / l2)
        ms = jnp.mean(o * o, axis=-1, keepdims=True)
        o_ref[0, :, cs] = (o * lax.rsqrt(ms + RMS_EPS) * g * out_scale).astype(o_ref.dtype)


def _attn(q, k1, k2, v, lam_row, g_row, out_scale):
    bsz, lp, w = q.shape
    tq = _row_tile(lp, 544)
    tk = 256
    qs = pl.BlockSpec((1, tq, w), lambda b, i: (b, i, 0))
    full = pl.BlockSpec((1, lp, w), lambda b, i: (b, 0, 0))
    row = pl.BlockSpec((1, 128), lambda b, i: (0, 0))
    return pl.pallas_call(
        functools.partial(_attn_kernel, tq=tq, tk=tk, out_scale=out_scale),
        grid=(bsz, lp // tq),
        in_specs=[qs, full, full, full, row, row],
        out_specs=qs,
        out_shape=jax.ShapeDtypeStruct((bsz, lp, w), BF16),
        scratch_shapes=[pltpu.VMEM((tq, 128), F32)] * 6 + [pltpu.VMEM((tq, tk), F32)] * 4,
        compiler_params=_params(("parallel", "arbitrary")),
        name="diff_attn",
    )(q, k1, k2, v, lam_row, g_row)


def _router_logits(hn, wr_hi_ref, wr_lo_ref, br_ref):
    x_hi, x_lo = _split2(hn)
    return (_dot(x_hi, wr_hi_ref[...]) + _dot(x_hi, wr_lo_ref[...]) + _dot(x_lo, wr_hi_ref[...])
            + br_ref[...])


def _router_group(logits):
    lane = lax.broadcasted_iota(jnp.int32, logits.shape, 1)
    is_g = lane < MOE_GROUPS
    glog = jnp.where(is_g, logits, MASK_VALUE)
    gmax = jnp.max(glog, axis=-1, keepdims=True)
    gsum = jnp.sum(jnp.where(is_g, jnp.exp(glog - gmax), 0.0), axis=-1, keepdims=True)
    gi = jnp.min(jnp.where(is_g & (glog == gmax), lane.astype(F32), 1e9), axis=-1, keepdims=True)
    return gi, 1.0 / gsum


def _router_gates(logits, gi, gp):
    lane = lax.broadcasted_iota(jnp.int32, logits.shape, 1)
    lanef = lane.astype(F32)
    big = 1e9
    e_idx = lane - MOE_GROUPS
    in_group = (e_idx >= 0) & (e_idx < MOE_EXPERTS) & ((e_idx >> 2) == gi)
    elog = jnp.where(in_group, logits, MASK_VALUE)
    v1 = jnp.max(elog, axis=-1, keepdims=True)
    i1 = jnp.min(jnp.where(in_group & (elog == v1), lanef, big), axis=-1, keepdims=True)
    rest = in_group & (lanef != i1)
    elog2 = jnp.where(rest, logits, MASK_VALUE)
    v2 = jnp.max(elog2, axis=-1, keepdims=True)
    i2 = jnp.min(jnp.where(rest & (elog2 == v2), lanef, big), axis=-1, keepdims=True)
    e21 = jnp.exp(v2 - v1)
    den = 1.0 / (1.0 + e21)
    return jnp.where(lanef == i1, gp * den, 0.0) + jnp.where(lanef == i2, gp * e21 * den, 0.0)


def _out_proj_kernel(h_ref, ys_ref, yr_ref, ya_ref, wo_ref, gn_ref, wr_hi_ref, wr_lo_ref, br_ref,
                     hout_ref, gi_ref, *, tl):
    acc = _dot(ys_ref[...].astype(BF16), wo_ref[0:256, :])
    acc += _dot(yr_ref[0].astype(BF16), wo_ref[256:512, :])
    acc += _dot(ya_ref[0], wo_ref[512:1024, :])
    h = _load_tokens(h_ref, tl) + acc
    rowpos = pl.program_id(1) * tl + lax.broadcasted_iota(jnp.int32, h.shape, 0)
    h = jnp.where(rowpos >= PAD, h, 0.0)
    _store_tokens(hout_ref, h)
    ms = jnp.mean(h * h, axis=-1, keepdims=True)
    hn = h * lax.rsqrt(ms + RMS_EPS) * gn_ref[...]
    gi, _ = _router_group(_router_logits(hn, wr_hi_ref, wr_lo_ref, br_ref))
    gi_ref[0] = gi.astype(jnp.int32)


def _out_proj(h_slab, bsz, lp, ys2d, yr, ya, wo_bf, gn, wr_hi, wr_lo, br):
    d = SLAB * 128
    tl = _row_tile(lp, 544)
    nl = lp // tl
    row = lambda b, i: (b, i, 0)
    const = lambda b, i: (0, 0)
    slab = pl.BlockSpec((tl * SLAB, 128), lambda b, i: (b * nl + i, 0))
    return pl.pallas_call(
        functools.partial(_out_proj_kernel, tl=tl),
        grid=(bsz, nl),
        in_specs=[slab,
                  pl.BlockSpec((tl, 256), lambda b, i: (i, b)),
                  pl.BlockSpec((1, tl, 256), row),
                  pl.BlockSpec((1, tl, 512), row),
                  pl.BlockSpec(wo_bf.shape, const),
                  pl.BlockSpec(gn.shape, const),
                  pl.BlockSpec(wr_hi.shape, const),
                  pl.BlockSpec(wr_lo.shape, const),
                  pl.BlockSpec(br.shape, const)],
        out_specs=(slab, pl.BlockSpec((1, tl, 1), row)),
        out_shape=(jax.ShapeDtypeStruct((bsz * lp * SLAB, 128), F32),
                   jax.ShapeDtypeStruct((bsz, lp, 1), jnp.int32)),
        compiler_params=_params(("parallel", "parallel")),
        name="out_proj_router",
    )(h_slab, ys2d, yr, ya, wo_bf, gn, wr_hi, wr_lo, br)


def _moe_kernel(grp_ref, used_ref, src_ref, srcn_ref, dst_ref, h_hbm, gn_ref, wr_hi_ref, wr_lo_ref, br_ref,
                wg_ref, wu_ref, wd_ref, out_hbm, xbuf, ybuf, gsem, ssem, *, tm):
    p = pl.program_id(0)
    used = used_ref[0]
    g = grp_ref[p]
    slot = p % 2

    def for_rows(start_row):
        def body(i, carry):
            for u in range(ROW_UNROLL):
                start_row(i * ROW_UNROLL + u, u % 2)
            return carry
        lax.fori_loop(0, tm // ROW_UNROLL, body, 0)

    def gather(idx_ref, into):
        def start_row(r, prio):
            rows = pl.ds(pl.multiple_of((into * tm + r) * SLAB, SLAB), SLAB)
            pltpu.make_async_copy(h_hbm.at[idx_ref[0, 0, r]], xbuf.at[rows, :],
                                  gsem.at[into]).start(priority=prio)
        for_rows(start_row)

    def scatter(dst_of_row):
        def start_row(r, prio):
            rows = pl.ds(pl.multiple_of(r * SLAB, SLAB), SLAB)
            pltpu.make_async_copy(ybuf.at[rows, :], out_hbm.at[dst_of_row(r)],
                                  ssem.at[0]).start(priority=prio)
        for_rows(start_row)

    def wait_tile(sem):
        pltpu.make_async_copy(h_hbm.at[pl.ds(0, tm)], out_hbm.at[pl.ds(0, tm)], sem).wait()

    @pl.when(p == 0)
    def _():
        gather(src_ref, 0)
        spare = out_hbm.shape[0] - tm
        ybuf[...] = jnp.zeros_like(ybuf)
        scatter(lambda r: spare + r)
        wait_tile(ssem.at[0])

    @pl.when(p < used)
    def _():
        wait_tile(gsem.at[slot])

        for s in range(2):
            @pl.when((p + 1 < used) & (slot == s))
            def _():
                gather(srcn_ref, 1 - s)

        h = _load_tokens(xbuf, tm, first=slot * tm)
        ms = jnp.mean(h * h, axis=-1, keepdims=True)
        hn = h * lax.rsqrt(ms + RMS_EPS) * gn_ref[...]
        logits = _router_logits(hn, wr_hi_ref, wr_lo_ref, br_ref)
        lane = lax.broadcasted_iota(jnp.int32, logits.shape, 1)
        is_g = lane < MOE_GROUPS
        glog = jnp.where(is_g, logits, MASK_VALUE)
        gmax = jnp.max(glog, axis=-1, keepdims=True)
        gsum = jnp.sum(jnp.where(is_g, jnp.exp(glog - gmax), 0.0), axis=-1, keepdims=True)
        lg = jnp.sum(jnp.where(lane == g, logits, 0.0), axis=-1, keepdims=True)
        gates = _router_gates(logits, g, jnp.exp(lg - gmax) / gsum)

        x = hn.astype(BF16)
        y = h
        for e in range(MOE_EXPERTS_PER_GROUP):
            ge = jnp.sum(jnp.where(lane == MOE_GROUPS + MOE_EXPERTS_PER_GROUP * g + e, gates, 0.0),
                         axis=-1, keepdims=True)
            a = _dot(x, wg_ref[e])
            b = _dot(x, wu_ref[e])
            he = (a * _sigmoid(a)) * b * ge
            y = y + _dot(he.astype(BF16), wd_ref[e])

        @pl.when(p > 0)
        def _():
            wait_tile(ssem.at[0])

        _store_tokens(ybuf, y)
        scatter(lambda r: dst_ref[0, 0, r])

        @pl.when(p + 1 == used)
        def _():
            wait_tile(ssem.at[0])


def _moe_routed(h_slab, gi, gn, wr_hi, wr_lo, br, wg, wu, wd, tm=512):
    t = gi.shape[0]
    ne, d, ff = wg.shape
    tm = _row_tile(t, tm)
    nt = t // tm + MOE_GROUPS

    order = jnp.argsort(gi, stable=True).astype(jnp.int32)
    counts = jnp.sum((gi[:, None] == jnp.arange(MOE_GROUPS, dtype=jnp.int32)[None, :]).astype(jnp.int32),
                     axis=0)
    starts = jnp.cumsum(counts) - counts
    tiles_g = (counts + tm - 1) // tm
    tile_end = jnp.cumsum(tiles_g)
    tile_start = tile_end - tiles_g
    pidx = jnp.arange(nt, dtype=jnp.int32)
    grp = jnp.minimum(jnp.sum((pidx[:, None] >= tile_end[None, :]).astype(jnp.int32), axis=1),
                      MOE_GROUPS - 1)
    j = pidx - tile_start[grp]
    nval = jnp.where(pidx < tile_end[-1], jnp.clip(counts[grp] - j * tm, 0, tm), 0)
    lane = jnp.arange(tm, dtype=jnp.int32)[None, :]
    rows = starts[grp][:, None] + j[:, None] * tm + lane
    src = order[jnp.clip(rows, 0, t - 1)]
    dst = jnp.where(lane < nval[:, None], src, t + lane)
    src = src.reshape(nt, 1, tm)
    dst = dst.reshape(nt, 1, tm).astype(jnp.int32)
    used = tile_end[-1:].astype(jnp.int32)

    const2 = lambda p, grp, used: (0, 0)
    wspec = lambda shape: pl.BlockSpec((MOE_EXPERTS_PER_GROUP,) + shape, lambda p, grp, used: (grp[p], 0, 0))
    idx_spec = lambda shift: pl.BlockSpec(
        (1, 1, tm), lambda p, grp, used: (jnp.minimum(p + shift, nt - 1), 0, 0), memory_space=pltpu.SMEM)
    grid_spec = pltpu.PrefetchScalarGridSpec(
        num_scalar_prefetch=2,
        grid=(nt,),
        in_specs=[idx_spec(0), idx_spec(1), idx_spec(0),
                  pl.BlockSpec(memory_space=pl.ANY),
                  pl.BlockSpec(gn.shape, const2),
                  pl.BlockSpec(wr_hi.shape, const2),
                  pl.BlockSpec(wr_lo.shape, const2),
                  pl.BlockSpec(br.shape, const2),
                  wspec((d, ff)), wspec((d, ff)), wspec((ff, d))],
        out_specs=pl.BlockSpec(memory_space=pl.ANY),
        scratch_shapes=[pltpu.VMEM((2 * tm * SLAB, 128), F32), pltpu.VMEM((tm * SLAB, 128), F32),
                        pltpu.SemaphoreType.DMA((2,)), pltpu.SemaphoreType.DMA((1,))],
    )
    n_in = h_slab.shape[0] // SLAB
    out = pl.pallas_call(
        functools.partial(_moe_kernel, tm=tm),
        grid_spec=grid_spec,
        out_shape=jax.ShapeDtypeStruct((t + tm, SLAB, 128), F32),
        compiler_params=_params(("arbitrary",)),
        name="moe_experts",
    )(grp, used, src, src, dst, h_slab.reshape(n_in, SLAB, 128), gn, wr_hi, wr_lo, br, wg, wu, wd)
    return out.reshape((t + tm) * SLAB, 128)


def _final_norm_kernel(h_ref, g_ref, o_ref, *, skip, chunk):
    g = g_ref[...]
    for c in range(o_ref.shape[1] // chunk):
        x = _load_tokens(h_ref, chunk, first=skip + c * chunk)
        ms = jnp.mean(x * x, axis=-1, keepdims=True)
        o_ref[0, c * chunk:(c + 1) * chunk, :] = x * lax.rsqrt(ms + RMS_EPS) * g


def _final_norm(h_slab, bsz, lp, g, seq):
    d = SLAB * 128
    chunk = _row_tile(seq, 512)
    return pl.pallas_call(
        functools.partial(_final_norm_kernel, skip=lp - seq, chunk=chunk),
        grid=(bsz,),
        in_specs=[pl.BlockSpec((lp * SLAB, 128), lambda b: (b, 0)),
                  pl.BlockSpec((1, d), lambda b: (0, 0))],
        out_specs=pl.BlockSpec((1, seq, d), lambda b: (b, 0, 0)),
        out_shape=jax.ShapeDtypeStruct((bsz, seq, d), F32),
        compiler_params=_params(("parallel",)),
        name="final_norm",
    )(h_slab, g.reshape(1, d))


def _s5_tables(lam_re, lam_im, log_dt, b_re, b_im, c_re, c_im):
    ng, ns = lam_re.shape
    lr = lam_re.astype(F32)
    li = lam_im.astype(F32)
    dt = jnp.exp(log_dt.astype(F32))[:, None]
    mag = jnp.exp(lr * dt)
    abar_r = mag * jnp.cos(li * dt)
    abar_i = mag * jnp.sin(li * dt)
    den = lr * lr + li * li
    zr = abar_r - 1.0
    zi = abar_i
    fr = (zr * lr + zi * li) / den
    fi = (zi * lr - zr * li) / den
    br = b_re.astype(F32)
    bi = b_im.astype(F32)
    bb_r = fr[..., None] * br - fi[..., None] * bi
    bb_i = fr[..., None] * bi + fi[..., None] * br
    eye = jnp.eye(ng, dtype=F32)
    nch = b_re.shape[2]
    to_b = lambda m: jnp.einsum('gnc,gh->gchn', m, eye).reshape(ng * nch, ng * ns)
    bbar = jnp.concatenate([to_b(bb_r), to_b(bb_i)], axis=1).astype(BF16)
    to_c = lambda m: jnp.einsum('gcn,gh->gnhc', m.astype(F32), eye).reshape(ng * ns, ng * nch)
    cmat = jnp.concatenate([to_c(c_re), -to_c(c_im)], axis=0).astype(BF16)
    a_row = jnp.concatenate([abar_r.reshape(1, -1), abar_i.reshape(1, -1)], axis=1)
    a8 = jnp.broadcast_to(a_row, (8, a_row.shape[1]))
    return bbar, a8, cmat


def _rope_tables(lp):
    half = 32
    inv_freq = 1.0 / (ROPE_THETA ** (jnp.arange(0, 2 * half, 2, dtype=F32) / (2 * half)))
    pos = jnp.arange(lp, dtype=F32) - float(PAD)
    ang = pos[:, None] * inv_freq[None, :]
    ang = jnp.concatenate([ang, ang, ang, ang], axis=-1)
    cos = jnp.cos(ang)
    sin = jnp.sin(ang)
    first = (jnp.arange(128) % 64) < half
    sina = jnp.where(first[None, :], -sin, 0.0)
    sinb = jnp.where(first[None, :], 0.0, sin)
    return cos, sina, sinb


def kernel(x, meta_tokens, norm_mix_g, w_in, s5_lambda_re, s5_lambda_im, s5_log_dt, s5_b_re, s5_b_im, s5_c_re, s5_c_im, s5_d, s5_w_glu, hgrn_lower_bounds, hgrn_norm_g, diff_lambda_q1, diff_lambda_k1, diff_lambda_q2, diff_lambda_k2, diff_subln_g, w_out, norm_ffn_g, moe_w_group, moe_b_group, moe_w_expert, moe_b_expert, moe_w_gate, moe_w_up, moe_w_down, final_norm_g):
    bsz, seq, d = x.shape
    depth = w_in.shape[0]
    lp = PAD + N_META + seq
    assert lp % 128 == 0 and bsz % 8 == 0

    assert d == SLAB * 128 and meta_tokens.shape[0] == N_META
    h = _embed(x.astype(F32), meta_tokens.astype(F32))

    cos, sina, sinb = _rope_tables(lp)
    lb_w = jax.nn.softmax(hgrn_lower_bounds.astype(F32), axis=0)
    lower_bounds = jnp.cumsum(lb_w, axis=0) - lb_w[0:1]

    for layer in range(depth):
        lam_init = 0.8 - 0.6 * math.exp(-0.3 * layer)
        u2d, hq, hf, hi, hg, aq, ak1, ak2, av = _in_proj(
            h, bsz, lp, norm_mix_g[layer].reshape(1, d).astype(F32), w_in[layer].astype(BF16), cos, sina, sinb)

        bbar, a8, cmat = _s5_tables(s5_lambda_re[layer], s5_lambda_im[layer], s5_log_dt[layer],
                                    s5_b_re[layer], s5_b_im[layer], s5_c_re[layer], s5_c_im[layer])
        y_ssm = _s5(u2d.reshape(lp, bsz, 256), bbar, a8, cmat,
                    s5_d[layer].reshape(1, -1).astype(F32), s5_w_glu[layer].astype(BF16))

        ng = jnp.tile(hgrn_norm_g[layer].astype(F32), HGRN_HEADS).reshape(1, -1)
        y_rec = _hgrn(hq, hf, hi, hg, lower_bounds[layer].reshape(1, -1), ng)

        lam = (jnp.exp(jnp.sum(diff_lambda_q1[layer].astype(F32) * diff_lambda_k1[layer].astype(F32)))
               - jnp.exp(jnp.sum(diff_lambda_q2[layer].astype(F32) * diff_lambda_k2[layer].astype(F32)))
               + lam_init)
        y_att = _attn(aq, ak1, ak2, av, jnp.full((1, 128), lam, F32),
                      diff_subln_g[layer].reshape(1, -1).astype(F32), 1.0 - lam_init)

        wr = jnp.concatenate([moe_w_group[layer].astype(F32), moe_w_expert[layer].astype(F32)], axis=1)
        wr = jnp.pad(wr, ((0, 0), (0, ROUTER_LANES - wr.shape[1])))
        wr_hi = wr.astype(BF16)
        wr_lo = (wr - wr_hi.astype(F32)).astype(BF16)
        br = jnp.concatenate([moe_b_group[layer].astype(F32), moe_b_expert[layer].astype(F32)])
        br = jnp.pad(br, (0, ROUTER_LANES - br.shape[0])).reshape(1, -1)
        gn = norm_ffn_g[layer].reshape(1, d).astype(F32)
        h, gi = _out_proj(h, bsz, lp, y_ssm.reshape(lp, bsz * 256), y_rec, y_att,
                          w_out[layer].astype(BF16), gn, wr_hi, wr_lo, br)
        h = _moe_routed(h, gi.reshape(bsz * lp), gn, wr_hi, wr_lo, br, moe_w_gate[layer].astype(BF16),
                        moe_w_up[layer].astype(BF16), moe_w_down[layer].astype(BF16))

    return _final_norm(h, bsz, lp, final_norm_g.astype(F32), seq)
```

```python
import functools
import math

import numpy as np
import jax
import jax.numpy as jnp
from jax import lax
from jax.experimental import pallas as pl
from jax.experimental.pallas import tpu as pltpu

F32 = jnp.float32
BF16 = jnp.bfloat16

N_META = 16
PAD = 112
RMS_EPS = 1e-6
MASK_VALUE = -1e30
LB_FLOOR = 1e-30
ROPE_THETA = 10000.0

S5_GROUP_CH = 16
S5_STATE = 64
HGRN_HEADS = 4
HGRN_CHUNK = 64
ATTN_HEADS = 4
ATTN_QK_DIM = 64
Q_SCALE = ATTN_QK_DIM ** -0.5 * math.log2(math.e)
MOE_GROUPS = 4
MOE_EXPERTS_PER_GROUP = 4
MOE_EXPERTS = MOE_GROUPS * MOE_EXPERTS_PER_GROUP
ROUTER_LANES = 128
ROW_UNROLL = 8

VMEM_LIMIT = 56 * 1024 * 1024


def _params(sem, vmem=VMEM_LIMIT):
    return pltpu.CompilerParams(dimension_semantics=sem, vmem_limit_bytes=vmem)


def _row_tile(n, cap, mult=16):
    best = None
    for t in range(mult, min(n, cap) + 1, mult):
        if n % t == 0:
            best = t
    assert best is not None, (n, cap)
    return best


def _sigmoid(x):
    return 1.0 / (1.0 + jnp.exp(-x))


def _dot(a, b):
    return jnp.dot(a, b, preferred_element_type=F32)


def _dot_nt(a, b):
    return lax.dot_general(a, b, (((1,), (1,)), ((), ())), preferred_element_type=F32)


def _dot_tn(a, b):
    return lax.dot_general(a, b, (((0,), (0,)), ((), ())), preferred_element_type=F32)


SLAB = 8


def _load_tokens(ref, n, first=0):
    return jnp.concatenate([ref[pl.ds(first * SLAB + s, n, stride=SLAB), :] for s in range(SLAB)], axis=1)


def _store_tokens(ref, val, first=0):
    n = val.shape[0]
    for s in range(SLAB):
        ref[pl.ds(first * SLAB + s, n, stride=SLAB), :] = val[:, s * 128:(s + 1) * 128]


def _embed_kernel(x_ref, meta_ref, o_ref, *, chunk):
    seq, d = x_ref.shape[1], x_ref.shape[2]
    _store_tokens(o_ref, jnp.zeros((PAD, d), F32))
    _store_tokens(o_ref, meta_ref[...], first=PAD)
    for c in range(seq // chunk):
        _store_tokens(o_ref, x_ref[0, c * chunk:(c + 1) * chunk, :], first=PAD + N_META + c * chunk)


def _embed(x, meta):
    bsz, seq, d = x.shape
    lp = PAD + N_META + seq
    return pl.pallas_call(
        functools.partial(_embed_kernel, chunk=_row_tile(seq, 512)),
        grid=(bsz,),
        in_specs=[pl.BlockSpec((1, seq, d), lambda b: (b, 0, 0)),
                  pl.BlockSpec(meta.shape, lambda b: (0, 0))],
        out_specs=pl.BlockSpec((lp * SLAB, 128), lambda b: (b, 0)),
        out_shape=jax.ShapeDtypeStruct((bsz * lp * SLAB, 128), F32),
        compiler_params=_params(("parallel",)),
        name="embed",
    )(x, meta)


def _split2(x):
    hi = x.astype(BF16)
    lo = (x - hi.astype(F32)).astype(BF16)
    return hi, lo


def _split3(x):
    hi = x.astype(BF16)
    r = x - hi.astype(F32)
    mid = r.astype(BF16)
    lo = (r - mid.astype(F32)).astype(BF16)
    return hi, mid, lo


def _in_proj_kernel(h_ref, g_ref, w_ref, cos_ref, sina_ref, sinb_ref,
                    u_ref, hq_ref, hf_ref, hi_ref, hg_ref, q_ref, k1_ref, k2_ref, v_ref):
    x = _load_tokens(h_ref, u_ref.shape[0])
    ms = jnp.mean(x * x, axis=-1, keepdims=True)
    xn = (x * lax.rsqrt(ms + RMS_EPS) * g_ref[...]).astype(BF16)

    def sec(lo, hi):
        return _dot(xn, w_ref[:, lo:hi])

    u_ref[...] = sec(0, 256)
    hq_ref[0] = sec(256, 512)
    hf_ref[0] = sec(512, 768)
    hi_ref[0] = sec(768, 1024)
    hg_ref[0] = sec(1024, 1280)

    cos = cos_ref[...]
    sina = sina_ref[...]
    sinb = sinb_ref[...]
    lane = lax.broadcasted_iota(jnp.int32, cos.shape, 1)
    first = lane < 64

    def rope(t):
        return t * cos + pltpu.roll(t, 96, 1) * sina + pltpu.roll(t, 32, 1) * sinb

    for pair in range(ATTN_HEADS // 2):
        qq = sec(1280 + pair * 256, 1536 + pair * 256)
        kk = sec(1792 + pair * 256, 2048 + pair * 256)
        for sub in range(2):
            cs = slice((2 * pair + sub) * 128, (2 * pair + sub + 1) * 128)
            q = rope(qq[:, sub * 128:(sub + 1) * 128]) * Q_SCALE
            q_ref[0, :, cs] = q.astype(BF16)
            k = rope(kk[:, sub * 128:(sub + 1) * 128])
            k1_ref[0, :, cs] = jnp.where(first, k, 0.0).astype(BF16)
            k2_ref[0, :, cs] = jnp.where(first, 0.0, k).astype(BF16)
    v_ref[0, :, 0:256] = sec(2304, 2560).astype(BF16)
    v_ref[0, :, 256:512] = sec(2560, 2816).astype(BF16)


def _in_proj(h_slab, bsz, lp, g, w_bf, cos, sina, sinb):
    d = SLAB * 128
    tl = _row_tile(lp, 544)
    nl = lp // tl
    row = lambda b, i: (b, i, 0)
    tab = pl.BlockSpec((tl, 128), lambda b, i: (i, 0))
    out_shape = (
        jax.ShapeDtypeStruct((lp, bsz * 256), F32),
        jax.ShapeDtypeStruct((bsz, lp, 256), F32),
        jax.ShapeDtypeStruct((bsz, lp, 256), F32),
        jax.ShapeDtypeStruct((bsz, lp, 256), F32),
        jax.ShapeDtypeStruct((bsz, lp, 256), F32),
        jax.ShapeDtypeStruct((bsz, lp, 512), BF16),
        jax.ShapeDtypeStruct((bsz, lp, 512), BF16),
        jax.ShapeDtypeStruct((bsz, lp, 512), BF16),
        jax.ShapeDtypeStruct((bsz, lp, 512), BF16),
    )
    s256 = pl.BlockSpec((1, tl, 256), row)
    s512 = pl.BlockSpec((1, tl, 512), row)
    return pl.pallas_call(
        _in_proj_kernel,
        grid=(bsz, nl),
        in_specs=[pl.BlockSpec((tl * SLAB, 128), lambda b, i: (b * nl + i, 0)),
                  pl.BlockSpec((1, d), lambda b, i: (0, 0)),
                  pl.BlockSpec(w_bf.shape, lambda b, i: (0, 0)),
                  tab, tab, tab],
        out_specs=(pl.BlockSpec((tl, 256), lambda b, i: (i, b)), s256, s256, s256, s256,
                   s512, s512, s512, s512),
        out_shape=out_shape,
        compiler_params=_params(("parallel", "parallel")),
        name="in_proj",
    )(h_slab, g, w_bf, cos, sina, sinb)


def _s5_kernel(u_ref, bbar_ref, a_ref, cmat_ref, d_ref, wglu_ref, y_ref, xs_ref, st_ref):
    tt = u_ref.shape[0]
    nst = a_ref.shape[1] // 2

    @pl.when(pl.program_id(1) == 0)
    def _():
        st_ref[...] = jnp.zeros_like(st_ref)

    u2 = u_ref[...].reshape(tt * 8, u_ref.shape[2])
    xs_ref[...] = _dot(u2.astype(BF16), bbar_ref[...])

    ar = a_ref[:, :nst]
    ai = a_ref[:, nst:]

    def step(t, carry):
        sr, si = carry
        r0 = pl.multiple_of(t * 8, 8)
        xr = xs_ref[pl.ds(r0, 8), :nst]
        xi = xs_ref[pl.ds(r0, 8), nst:]
        nr = ar * sr - ai * si + xr
        ni = ar * si + ai * sr + xi
        xs_ref[pl.ds(r0, 8), :nst] = nr
        xs_ref[pl.ds(r0, 8), nst:] = ni
        return nr, ni

    sr, si = lax.fori_loop(0, tt, step, (st_ref[:, :nst], st_ref[:, nst:]))
    st_ref[:, :nst] = sr
    st_ref[:, nst:] = si

    y = _dot(xs_ref[...].astype(BF16), cmat_ref[...]) + d_ref[...] * u2
    g = 0.5 * y * (1.0 + jnp.tanh(0.7978845608028654 * (y + 0.044715 * (y * y * y))))
    gl = _dot(g.astype(BF16), wglu_ref[...])
    w = gl.shape[1] // 2
    out = gl[:, :w] * _sigmoid(gl[:, w:])
    y_ref[...] = out.reshape(y_ref.shape)


def _s5(u3, bbar, a8, cmat, dskip, wglu):
    lp, bsz, w = u3.shape
    tt = 128
    nst2 = a8.shape[1]
    const = lambda bg, i: (0, 0)
    return pl.pallas_call(
        _s5_kernel,
        grid=(bsz // 8, lp // tt),
        in_specs=[pl.BlockSpec((tt, 8, w), lambda bg, i: (i, bg, 0)),
                  pl.BlockSpec(bbar.shape, const),
                  pl.BlockSpec(a8.shape, const),
                  pl.BlockSpec(cmat.shape, const),
                  pl.BlockSpec(dskip.shape, const),
                  pl.BlockSpec(wglu.shape, const)],
        out_specs=pl.BlockSpec((tt, 8, w), lambda bg, i: (i, bg, 0)),
        out_shape=jax.ShapeDtypeStruct((lp, bsz, w), F32),
        scratch_shapes=[pltpu.VMEM((tt * 8, nst2), F32), pltpu.VMEM((8, nst2), F32)],
        compiler_params=_params(("parallel", "arbitrary")),
        name="s5_scan",
    )(u3, bbar, a8, cmat, dskip, wglu)


_LEVEL_HALVES = (32, 16, 8, 4, 2, 1)


def _hgrn_constants():
    c = HGRN_CHUNK
    t = np.arange(c)[:, None]
    j = np.arange(c)[None, :]
    mats = [(j <= t), (j > t)]
    for half in _LEVEL_HALVES:
        upper = (t & half) != 0
        seg_lo = t - (t % half)
        seg_hi = t | (half - 1)
        mats.append(np.where(upper, (j >= seg_lo) & (j <= t), (j > t) & (j <= seg_hi)))
    dmat = np.concatenate(mats, axis=0).astype(np.float32)
    s = np.arange(4 * c)[None, :] % c
    x = t ^ s
    hb = np.floor(np.log2(np.maximum(x, 1))).astype(np.int32)
    lv = np.where(s > t, -1, np.where(s == t, 6, hb)).astype(np.int32)
    hm = (np.arange(4 * c)[:, None] // c == np.arange(4 * c)[None, :] // c).astype(np.float32)
    return dmat, lv, hm


def _hgrn_kernel(zq_ref, zf_ref, zi_ref, zg_ref, lb_ref, ng_ref, dmat_ref, lv_ref, hm_ref,
                 o_ref, st_ref):
    c = HGRN_CHUNK
    w = zq_ref.shape[2]
    nchunks = zq_ref.shape[1] // c

    lb = lb_ref[...]
    lbm = jnp.maximum(lb, LB_FLOOR)
    one_m_lb = 1.0 - lb
    k_off = lb - lbm
    ng = ng_ref[...]
    row = lax.broadcasted_iota(jnp.int32, (c, w), 0)

    def chunk(ci, st):
        r0 = pl.multiple_of(ci * c, c)
        zq = zq_ref[0, pl.ds(r0, c), :]
        zf = zf_ref[0, pl.ds(r0, c), :]
        v = zi_ref[0, pl.ds(r0, c), :]
        zg = zg_ref[0, pl.ds(r0, c), :]
        hm = hm_ref[...]
        hmb = hm.astype(BF16)
        lv = lv_ref[...]

        q = zq * _sigmoid(zq)
        logf = jnp.log(lbm + one_m_lb * _sigmoid(zf))
        k = one_m_lb * _sigmoid(-zf) + k_off

        dmat = dmat_ref[...]
        hi, mid, lo = _split3(logf)
        e_all = jnp.exp(_dot(dmat, hi) + _dot(dmat, mid) + _dot(dmat, lo))
        e_cum = e_all[0:c]
        e_rest = e_all[c:2 * c]

        p = jnp.zeros((c, 4 * c), F32)
        for li, half in enumerate(_LEVEL_HALVES):
            e_l = e_all[(2 + li) * c:(3 + li) * c]
            upper = (row & half) != 0
            q_l = jnp.where(upper, q * e_l, 0.0).astype(BF16)
            k_l = jnp.where(upper, 0.0, k * e_l).astype(BF16)
            r_l = jnp.concatenate([k_l] * HGRN_HEADS, axis=0) * hmb
            s_l = _dot_nt(q_l, r_l)
            p = jnp.where(lv == int(math.log2(half)), s_l, p)
        s_d = _dot((q * k).astype(BF16), hmb)
        p = jnp.where(lv == 6, s_d, p)

        vb = v.astype(BF16)
        vbd = jnp.concatenate([vb] * HGRN_HEADS, axis=0) * hmb
        o = _dot(p.astype(BF16), vbd) + _dot_nt((q * e_cum).astype(BF16), st.astype(BF16))

        upd = _dot_tn(vb, (k * e_rest).astype(BF16))
        st_new = st * e_cum[c - 1:c, :] + upd * hm

        oo_hi, oo_lo = _split2(o * o)
        ms = (_dot(oo_hi, hmb) + _dot(oo_lo, hmb)) * (1.0 / (w // HGRN_HEADS))
        out = o * lax.rsqrt(ms + RMS_EPS) * ng * (zg * _sigmoid(zg))
        o_ref[0, pl.ds(r0, c), :] = out
        return st_new

    lax.fori_loop(0, nchunks, chunk, jnp.zeros((w, w), F32), unroll=2)


def _hgrn(zq, zf, zi, zg, lb, ng):
    bsz, lp, w = zq.shape
    dmat, lv, hm = _hgrn_constants()
    dmat = jnp.asarray(dmat, BF16)
    lv = jnp.asarray(lv)
    hm = jnp.asarray(hm, F32)
    seq = pl.BlockSpec((1, lp, w), lambda b: (b, 0, 0))
    const = lambda b: (0, 0)
    return pl.pallas_call(
        _hgrn_kernel,
        grid=(bsz,),
        in_specs=[seq, seq, seq, seq,
                  pl.BlockSpec(lb.shape, const), pl.BlockSpec(ng.shape, const),
                  pl.BlockSpec(dmat.shape, const), pl.BlockSpec(lv.shape, const),
                  pl.BlockSpec(hm.shape, const)],
        out_specs=seq,
        out_shape=jax.ShapeDtypeStruct((bsz, lp, w), F32),
        scratch_shapes=[pltpu.VMEM((w, w), F32)],
        compiler_params=_params(("parallel",)),
        name="hgrn2",
    )(zq, zf, zi, zg, lb, ng, dmat, lv, hm)


def _attn_kernel(q_ref, k1_ref, k2_ref, v_ref, lam_ref, g_ref, o_ref, m1_ref, l1_ref, acc1_ref,
                 m2_ref, l2_ref, acc2_ref, sa1_ref, sa2_ref, sb1_ref, sb2_ref, *, tq, tk, out_scale):
    state = ((m1_ref, l1_ref, acc1_ref), (m2_ref, l2_ref, acc2_ref))
    s_a = (sa1_ref, sa2_ref)
    s_b = (sb1_ref, sb2_ref)
    i = pl.program_id(1)
    lp = k1_ref.shape[1]
    q0 = i * tq
    nkv = (q0 + tq + tk - 1) // tk
    kloc = lax.broadcasted_iota(jnp.int32, (tq, tk), 1)
    k_minus_q = kloc - lax.broadcasted_iota(jnp.int32, (tq, tk), 0)
    nrep = tk // 128
    lam = lam_ref[...]
    g = g_ref[...]

    for hd in range(ATTN_HEADS):
        cs = slice(hd * 128, (hd + 1) * 128)
        qh = q_ref[0, :, cs]
        for m_ref, l_ref, acc_ref in state:
            m_ref[...] = jnp.full(m_ref.shape, MASK_VALUE, F32)
            l_ref[...] = jnp.zeros(l_ref.shape, F32)
            acc_ref[...] = jnp.zeros(acc_ref.shape, F32)

        def key_start(j):
            return pl.multiple_of(jnp.minimum(j * tk, lp - tk), 128)

        def scores(j, into):
            k0 = key_start(j)
            for kr, s_ref in zip((k1_ref, k2_ref), into):
                s_ref[...] = _dot_nt(qh, kr[0, pl.ds(k0, tk), cs])

        def block(j, cur, nxt):
            if nxt is not None:
                scores(j + 1, nxt)
            k0 = key_start(j)
            vb = v_ref[0, pl.ds(k0, tk), cs]
            valid = (k_minus_q <= q0 - k0) & (kloc >= jnp.maximum(PAD, j * tk) - k0)
            for s_ref, (m_ref, l_ref, acc_ref) in zip(cur, state):
                s = jnp.where(valid, s_ref[...], MASK_VALUE)
                m_prev = m_ref[...]
                m_next = jnp.maximum(m_prev, jnp.max(s, axis=-1, keepdims=True))
                alpha = jnp.exp2(m_prev - m_next)
                p = jnp.exp2(s - jnp.concatenate([m_next] * nrep, axis=1))
                psum = p[:, 0:128]
                for r in range(1, nrep):
                    psum = psum + p[:, r * 128:(r + 1) * 128]
                m_ref[...] = m_next
                l_ref[...] = alpha * l_ref[...] + psum
                acc_ref[...] = alpha * acc_ref[...] + _dot(p.astype(BF16), vb)

        def pair(t, carry):
            block(2 * t, s_a, s_b)
            block(2 * t + 1, s_b, s_a)
            return carry

        scores(0, s_a)
        lax.fori_loop(0, nkv // 2, pair, 0)

        @pl.when(nkv % 2 == 1)
        def _():
            block(nkv - 1, s_a, None)

        l1 = jnp.sum(l1_ref[...], axis=-1, keepdims=True)
        l2 = jnp.sum(l2_ref[...], axis=-1, keepdims=True)
        o = acc1_ref[...] / l1 - lam * (acc2_ref[...] / l2)
        ms = jnp.mean(o * o, axis=-1, keepdims=True)
        o_ref[0, :, cs] = (o * lax.rsqrt(ms + RMS_EPS) * g * out_scale).astype(o_ref.dtype)


def _attn(q, k1, k2, v, lam_row, g_row, out_scale):
    bsz, lp, w = q.shape
    tq = _row_tile(lp, 544)
    tk = 256
    qs = pl.BlockSpec((1, tq, w), lambda b, i: (b, i, 0))
    full = pl.BlockSpec((1, lp, w), lambda b, i: (b, 0, 0))
    row = pl.BlockSpec((1, 128), lambda b, i: (0, 0))
    return pl.pallas_call(
        functools.partial(_attn_kernel, tq=tq, tk=tk, out_scale=out_scale),
        grid=(bsz, lp // tq),
        in_specs=[qs, full, full, full, row, row],
        out_specs=qs,
        out_shape=jax.ShapeDtypeStruct((bsz, lp, w), BF16),
        scratch_shapes=[pltpu.VMEM((tq, 128), F32)] * 6 + [pltpu.VMEM((tq, tk), F32)] * 4,
        compiler_params=_params(("parallel", "arbitrary")),
        name="diff_attn",
    )(q, k1, k2, v, lam_row, g_row)


def _router_logits(hn, wr_ref, br_ref):
    x_hi, x_lo = _split2(hn)
    both = _dot(x_hi, wr_ref[...])
    return (both[:, :ROUTER_LANES] + both[:, ROUTER_LANES:] + _dot(x_lo, wr_ref[:, :ROUTER_LANES])
            + br_ref[...])


def _router_group(logits):
    lane = lax.broadcasted_iota(jnp.int32, logits.shape, 1)
    is_g = lane < MOE_GROUPS
    glog = jnp.where(is_g, logits, MASK_VALUE)
    gmax = jnp.max(glog, axis=-1, keepdims=True)
    gsum = jnp.sum(jnp.where(is_g, jnp.exp(glog - gmax), 0.0), axis=-1, keepdims=True)
    gi = jnp.min(jnp.where(is_g & (glog == gmax), lane.astype(F32), 1e9), axis=-1, keepdims=True)
    return gi, 1.0 / gsum


def _router_gates(logits, gi, gp):
    lane = lax.broadcasted_iota(jnp.int32, logits.shape, 1)
    lanef = lane.astype(F32)
    big = 1e9
    e_idx = lane - MOE_GROUPS
    in_group = (e_idx >= 0) & (e_idx < MOE_EXPERTS) & ((e_idx >> 2) == gi)
    elog = jnp.where(in_group, logits, MASK_VALUE)
    v1 = jnp.max(elog, axis=-1, keepdims=True)
    i1 = jnp.min(jnp.where(in_group & (elog == v1), lanef, big), axis=-1, keepdims=True)
    rest = in_group & (lanef != i1)
    elog2 = jnp.where(rest, logits, MASK_VALUE)
    v2 = jnp.max(elog2, axis=-1, keepdims=True)
    i2 = jnp.min(jnp.where(rest & (elog2 == v2), lanef, big), axis=-1, keepdims=True)
    e21 = jnp.exp(v2 - v1)
    den = 1.0 / (1.0 + e21)
    return jnp.where(lanef == i1, gp * den, 0.0) + jnp.where(lanef == i2, gp * e21 * den, 0.0)


def _out_proj_kernel(h_ref, ys_ref, yr_ref, ya_ref, wo_ref, gn_ref, wr_ref, br_ref,
                     hout_ref, gi_ref, *, tl):
    acc = _dot(ys_ref[...].astype(BF16), wo_ref[0:256, :])
    acc += _dot(yr_ref[0].astype(BF16), wo_ref[256:512, :])
    acc += _dot(ya_ref[0], wo_ref[512:1024, :])
    h = _load_tokens(h_ref, tl) + acc
    rowpos = pl.program_id(1) * tl + lax.broadcasted_iota(jnp.int32, h.shape, 0)
    h = jnp.where(rowpos >= PAD, h, 0.0)
    _store_tokens(hout_ref, h)
    ms = jnp.mean(h * h, axis=-1, keepdims=True)
    hn = h * lax.rsqrt(ms + RMS_EPS) * gn_ref[...]
    gi, _ = _router_group(_router_logits(hn, wr_ref, br_ref))
    gi_ref[0] = gi.astype(jnp.int32)


def _out_proj(h_slab, bsz, lp, ys2d, yr, ya, wo_bf, gn, wr, br):
    d = SLAB * 128
    tl = _row_tile(lp, 544)
    nl = lp // tl
    row = lambda b, i: (b, i, 0)
    const = lambda b, i: (0, 0)
    slab = pl.BlockSpec((tl * SLAB, 128), lambda b, i: (b * nl + i, 0))
    return pl.pallas_call(
        functools.partial(_out_proj_kernel, tl=tl),
        grid=(bsz, nl),
        in_specs=[slab,
                  pl.BlockSpec((tl, 256), lambda b, i: (i, b)),
                  pl.BlockSpec((1, tl, 256), row),
                  pl.BlockSpec((1, tl, 512), row),
                  pl.BlockSpec(wo_bf.shape, const),
                  pl.BlockSpec(gn.shape, const),
                  pl.BlockSpec(wr.shape, const),
                  pl.BlockSpec(br.shape, const)],
        out_specs=(slab, pl.BlockSpec((1, tl, 1), row)),
        out_shape=(jax.ShapeDtypeStruct((bsz * lp * SLAB, 128), F32),
                   jax.ShapeDtypeStruct((bsz, lp, 1), jnp.int32)),
        compiler_params=_params(("parallel", "parallel")),
        name="out_proj_router",
    )(h_slab, ys2d, yr, ya, wo_bf, gn, wr, br)


def _moe_kernel(grp_ref, used_ref, src_ref, srcn_ref, dst_ref, h_hbm, gn_ref, wr_ref, br_ref,
                wg_ref, wu_ref, wd_ref, out_hbm, xbuf, ybuf, gsem, ssem, *, tm):
    p = pl.program_id(0)
    used = used_ref[0]
    g = grp_ref[p]
    slot = p % 2

    def for_rows(start_row):
        def body(i, carry):
            for u in range(ROW_UNROLL):
                start_row(i * ROW_UNROLL + u, u % 2)
            return carry
        lax.fori_loop(0, tm // ROW_UNROLL, body, 0)

    def gather(idx_ref, into):
        def start_row(r, prio):
            rows = pl.ds(pl.multiple_of((into * tm + r) * SLAB, SLAB), SLAB)
            pltpu.make_async_copy(h_hbm.at[idx_ref[0, 0, r]], xbuf.at[rows, :],
                                  gsem.at[into]).start(priority=prio)
        for_rows(start_row)

    def scatter(dst_of_row):
        def start_row(r, prio):
            rows = pl.ds(pl.multiple_of(r * SLAB, SLAB), SLAB)
            pltpu.make_async_copy(ybuf.at[rows, :], out_hbm.at[dst_of_row(r)],
                                  ssem.at[0]).start(priority=prio)
        for_rows(start_row)

    def wait_tile(sem):
        pltpu.make_async_copy(h_hbm.at[pl.ds(0, tm)], out_hbm.at[pl.ds(0, tm)], sem).wait()

    @pl.when(p == 0)
    def _():
        gather(src_ref, 0)
        spare = out_hbm.shape[0] - tm
        ybuf[...] = jnp.zeros_like(ybuf)
        scatter(lambda r: spare + r)
        wait_tile(ssem.at[0])

    @pl.when(p < used)
    def _():
        wait_tile(gsem.at[slot])

        for s in range(2):
            @pl.when((p + 1 < used) & (slot == s))
            def _():
                gather(srcn_ref, 1 - s)

        h = _load_tokens(xbuf, tm, first=slot * tm)
        ms = jnp.mean(h * h, axis=-1, keepdims=True)
        hn = h * lax.rsqrt(ms + RMS_EPS) * gn_ref[...]
        logits = _router_logits(hn, wr_ref, br_ref)
        lane = lax.broadcasted_iota(jnp.int32, logits.shape, 1)
        is_g = lane < MOE_GROUPS
        glog = jnp.where(is_g, logits, MASK_VALUE)
        gmax = jnp.max(glog, axis=-1, keepdims=True)
        gsum = jnp.sum(jnp.where(is_g, jnp.exp(glog - gmax), 0.0), axis=-1, keepdims=True)
        lg = jnp.sum(jnp.where(lane == g, logits, 0.0), axis=-1, keepdims=True)
        gates = _router_gates(logits, g, jnp.exp(lg - gmax) / gsum)

        x = hn.astype(BF16)
        y = h
        for e in range(MOE_EXPERTS_PER_GROUP):
            ge = jnp.sum(jnp.where(lane == MOE_GROUPS + MOE_EXPERTS_PER_GROUP * g + e, gates, 0.0),
                         axis=-1, keepdims=True)
            a = _dot(x, wg_ref[e])
            b = _dot(x, wu_ref[e])
            he = (a * _sigmoid(a)) * b * ge
            y = y + _dot(he.astype(BF16), wd_ref[e])

        @pl.when(p > 0)
        def _():
            wait_tile(ssem.at[0])

        _store_tokens(ybuf, y)
        scatter(lambda r: dst_ref[0, 0, r])

        @pl.when(p + 1 == used)
        def _():
            wait_tile(ssem.at[0])


def _moe_routed(h_slab, gi, gn, wr, br, wg, wu, wd, tm=512):
    t = gi.shape[0]
    ne, d, ff = wg.shape
    tm = _row_tile(t, tm)
    nt = t // tm + MOE_GROUPS

    order = jnp.argsort(gi, stable=True).astype(jnp.int32)
    counts = jnp.sum((gi[:, None] == jnp.arange(MOE_GROUPS, dtype=jnp.int32)[None, :]).astype(jnp.int32),
                     axis=0)
    starts = jnp.cumsum(counts) - counts
    tiles_g = (counts + tm - 1) // tm
    tile_end = jnp.cumsum(tiles_g)
    tile_start = tile_end - tiles_g
    pidx = jnp.arange(nt, dtype=jnp.int32)
    grp = jnp.minimum(jnp.sum((pidx[:, None] >= tile_end[None, :]).astype(jnp.int32), axis=1),
                      MOE_GROUPS - 1)
    j = pidx - tile_start[grp]
    nval = jnp.where(pidx < tile_end[-1], jnp.clip(counts[grp] - j * tm, 0, tm), 0)
    lane = jnp.arange(tm, dtype=jnp.int32)[None, :]
    rows = starts[grp][:, None] + j[:, None] * tm + lane
    src = order[jnp.clip(rows, 0, t - 1)]
    dst = jnp.where(lane < nval[:, None], src, t + lane)
    src = src.reshape(nt, 1, tm)
    dst = dst.reshape(nt, 1, tm).astype(jnp.int32)
    used = tile_end[-1:].astype(jnp.int32)

    const2 = lambda p, grp, used: (0, 0)
    wspec = lambda shape: pl.BlockSpec((MOE_EXPERTS_PER_GROUP,) + shape, lambda p, grp, used: (grp[p], 0, 0))
    idx_spec = lambda shift: pl.BlockSpec(
        (1, 1, tm), lambda p, grp, used: (jnp.minimum(p + shift, nt - 1), 0, 0), memory_space=pltpu.SMEM)
    grid_spec = pltpu.PrefetchScalarGridSpec(
        num_scalar_prefetch=2,
        grid=(nt,),
        in_specs=[idx_spec(0), idx_spec(1), idx_spec(0),
                  pl.BlockSpec(memory_space=pl.ANY),
                  pl.BlockSpec(gn.shape, const2),
                  pl.BlockSpec(wr.shape, const2),
                  pl.BlockSpec(br.shape, const2),
                  wspec((d, ff)), wspec((d, ff)), wspec((ff, d))],
        out_specs=pl.BlockSpec(memory_space=pl.ANY),
        scratch_shapes=[pltpu.VMEM((2 * tm * SLAB, 128), F32), pltpu.VMEM((tm * SLAB, 128), F32),
                        pltpu.SemaphoreType.DMA((2,)), pltpu.SemaphoreType.DMA((1,))],
    )
    n_in = h_slab.shape[0] // SLAB
    out = pl.pallas_call(
        functools.partial(_moe_kernel, tm=tm),
        grid_spec=grid_spec,
        out_shape=jax.ShapeDtypeStruct((t + tm, SLAB, 128), F32),
        compiler_params=_params(("arbitrary",)),
        name="moe_experts",
    )(grp, used, src, src, dst, h_slab.reshape(n_in, SLAB, 128), gn, wr, br, wg, wu, wd)
    return out.reshape((t + tm) * SLAB, 128)


def _final_norm_kernel(h_ref, g_ref, o_ref, *, skip, chunk):
    g = g_ref[...]
    for c in range(o_ref.shape[1] // chunk):
        x = _load_tokens(h_ref, chunk, first=skip + c * chunk)
        ms = jnp.mean(x * x, axis=-1, keepdims=True)
        o_ref[0, c * chunk:(c + 1) * chunk, :] = x * lax.rsqrt(ms + RMS_EPS) * g


def _final_norm(h_slab, bsz, lp, g, seq):
    d = SLAB * 128
    chunk = _row_tile(seq, 512)
    return pl.pallas_call(
        functools.partial(_final_norm_kernel, skip=lp - seq, chunk=chunk),
        grid=(bsz,),
        in_specs=[pl.BlockSpec((lp * SLAB, 128), lambda b: (b, 0)),
                  pl.BlockSpec((1, d), lambda b: (0, 0))],
        out_specs=pl.BlockSpec((1, seq, d), lambda b: (b, 0, 0)),
        out_shape=jax.ShapeDtypeStruct((bsz, seq, d), F32),
        compiler_params=_params(("parallel",)),
        name="final_norm",
    )(h_slab, g.reshape(1, d))


def _s5_tables(lam_re, lam_im, log_dt, b_re, b_im, c_re, c_im):
    ng, ns = lam_re.shape
    lr = lam_re.astype(F32)
    li = lam_im.astype(F32)
    dt = jnp.exp(log_dt.astype(F32))[:, None]
    mag = jnp.exp(lr * dt)
    abar_r = mag * jnp.cos(li * dt)
    abar_i = mag * jnp.sin(li * dt)
    den = lr * lr + li * li
    zr = abar_r - 1.0
    zi = abar_i
    fr = (zr * lr + zi * li) / den
    fi = (zi * lr - zr * li) / den
    br = b_re.astype(F32)
    bi = b_im.astype(F32)
    bb_r = fr[..., None] * br - fi[..., None] * bi
    bb_i = fr[..., None] * bi + fi[..., None] * br
    eye = jnp.eye(ng, dtype=F32)
    nch = b_re.shape[2]
    to_b = lambda m: jnp.einsum('gnc,gh->gchn', m, eye).reshape(ng * nch, ng * ns)
    bbar = jnp.concatenate([to_b(bb_r), to_b(bb_i)], axis=1).astype(BF16)
    to_c = lambda m: jnp.einsum('gcn,gh->gnhc', m.astype(F32), eye).reshape(ng * ns, ng * nch)
    cmat = jnp.concatenate([to_c(c_re), -to_c(c_im)], axis=0).astype(BF16)
    a_row = jnp.concatenate([abar_r.reshape(1, -1), abar_i.reshape(1, -1)], axis=1)
    a8 = jnp.broadcast_to(a_row, (8, a_row.shape[1]))
    return bbar, a8, cmat


def _rope_tables(lp):
    half = 32
    inv_freq = 1.0 / (ROPE_THETA ** (jnp.arange(0, 2 * half, 2, dtype=F32) / (2 * half)))
    pos = jnp.arange(lp, dtype=F32) - float(PAD)
    ang = pos[:, None] * inv_freq[None, :]
    ang = jnp.concatenate([ang, ang, ang, ang], axis=-1)
    cos = jnp.cos(ang)
    sin = jnp.sin(ang)
    first = (jnp.arange(128) % 64) < half
    sina = jnp.where(first[None, :], -sin, 0.0)
    sinb = jnp.where(first[None, :], 0.0, sin)
    return cos, sina, sinb


def kernel(x, meta_tokens, norm_mix_g, w_in, s5_lambda_re, s5_lambda_im, s5_log_dt, s5_b_re, s5_b_im, s5_c_re, s5_c_im, s5_d, s5_w_glu, hgrn_lower_bounds, hgrn_norm_g, diff_lambda_q1, diff_lambda_k1, diff_lambda_q2, diff_lambda_k2, diff_subln_g, w_out, norm_ffn_g, moe_w_group, moe_b_group, moe_w_expert, moe_b_expert, moe_w_gate, moe_w_up, moe_w_down, final_norm_g):
    bsz, seq, d = x.shape
    depth = w_in.shape[0]
    lp = PAD + N_META + seq
    assert lp % 128 == 0 and bsz % 8 == 0

    assert d == SLAB * 128 and meta_tokens.shape[0] == N_META
    h = _embed(x.astype(F32), meta_tokens.astype(F32))

    cos, sina, sinb = _rope_tables(lp)
    lb_w = jax.nn.softmax(hgrn_lower_bounds.astype(F32), axis=0)
    lower_bounds = jnp.cumsum(lb_w, axis=0) - lb_w[0:1]

    for layer in range(depth):
        lam_init = 0.8 - 0.6 * math.exp(-0.3 * layer)
        u2d, hq, hf, hi, hg, aq, ak1, ak2, av = _in_proj(
            h, bsz, lp, norm_mix_g[layer].reshape(1, d).astype(F32), w_in[layer].astype(BF16), cos, sina, sinb)

        bbar, a8, cmat = _s5_tables(s5_lambda_re[layer], s5_lambda_im[layer], s5_log_dt[layer],
                                    s5_b_re[layer], s5_b_im[layer], s5_c_re[layer], s5_c_im[layer])
        y_ssm = _s5(u2d.reshape(lp, bsz, 256), bbar, a8, cmat,
                    s5_d[layer].reshape(1, -1).astype(F32), s5_w_glu[layer].astype(BF16))

        ng = jnp.tile(hgrn_norm_g[layer].astype(F32), HGRN_HEADS).reshape(1, -1)
        y_rec = _hgrn(hq, hf, hi, hg, lower_bounds[layer].reshape(1, -1), ng)

        lam = (jnp.exp(jnp.sum(diff_lambda_q1[layer].astype(F32) * diff_lambda_k1[layer].astype(F32)))
               - jnp.exp(jnp.sum(diff_lambda_q2[layer].astype(F32) * diff_lambda_k2[layer].astype(F32)))
               + lam_init)
        y_att = _attn(aq, ak1, ak2, av, jnp.full((1, 128), lam, F32),
                      diff_subln_g[layer].reshape(1, -1).astype(F32), 1.0 - lam_init)

        wr = jnp.concatenate([moe_w_group[layer].astype(F32), moe_w_expert[layer].astype(F32)], axis=1)
        wr = jnp.pad(wr, ((0, 0), (0, ROUTER_LANES - wr.shape[1])))
        wr_hi = wr.astype(BF16)
        wr = jnp.concatenate([wr_hi, (wr - wr_hi.astype(F32)).astype(BF16)], axis=1)
        br = jnp.concatenate([moe_b_group[layer].astype(F32), moe_b_expert[layer].astype(F32)])
        br = jnp.pad(br, (0, ROUTER_LANES - br.shape[0])).reshape(1, -1)
        gn = norm_ffn_g[layer].reshape(1, d).astype(F32)
        h, gi = _out_proj(h, bsz, lp, y_ssm.reshape(lp, bsz * 256), y_rec, y_att,
                          w_out[layer].astype(BF16), gn, wr, br)
        h = _moe_routed(h, gi.reshape(bsz * lp), gn, wr, br, moe_w_gate[layer].astype(BF16),
                        moe_w_up[layer].astype(BF16), moe_w_down[layer].astype(BF16))

    return _final_norm(h, bsz, lp, final_norm_g.astype(F32), seq)
```

```python
import functools
import math

import numpy as np
import jax
import jax.numpy as jnp
from jax import lax
from jax.experimental import pallas as pl
from jax.experimental.pallas import tpu as pltpu

F32 = jnp.float32
BF16 = jnp.bfloat16

N_META = 16
PAD = 112
RMS_EPS = 1e-6
MASK_VALUE = -1e30
LB_FLOOR = 1e-30
ROPE_THETA = 10000.0

S5_GROUP_CH = 16
S5_STATE = 64
HGRN_HEADS = 4
HGRN_CHUNK = 64
HGRN_SEQS_PER_STEP = 4
ATTN_HEADS = 4
ATTN_QK_DIM = 64
Q_SCALE = ATTN_QK_DIM ** -0.5 * math.log2(math.e)
MOE_GROUPS = 4
MOE_EXPERTS_PER_GROUP = 4
MOE_EXPERTS = MOE_GROUPS * MOE_EXPERTS_PER_GROUP
ROUTER_LANES = 128
ROW_UNROLL = 8

VMEM_LIMIT = 56 * 1024 * 1024


def _params(sem, vmem=VMEM_LIMIT):
    return pltpu.CompilerParams(dimension_semantics=sem, vmem_limit_bytes=vmem)


def _row_tile(n, cap, mult=16):
    best = None
    for t in range(mult, min(n, cap) + 1, mult):
        if n % t == 0:
            best = t
    assert best is not None, (n, cap)
    return best


def _sigmoid(x):
    return 1.0 / (1.0 + jnp.exp(-x))


def _dot(a, b):
    return jnp.dot(a, b, preferred_element_type=F32)


def _dot_nt(a, b):
    return lax.dot_general(a, b, (((1,), (1,)), ((), ())), preferred_element_type=F32)


def _dot_tn(a, b):
    return lax.dot_general(a, b, (((0,), (0,)), ((), ())), preferred_element_type=F32)


SLAB = 8


def _load_tokens(ref, n, first=0):
    return jnp.concatenate([ref[pl.ds(first * SLAB + s, n, stride=SLAB), :] for s in range(SLAB)], axis=1)


def _store_tokens(ref, val, first=0):
    n = val.shape[0]
    for s in range(SLAB):
        ref[pl.ds(first * SLAB + s, n, stride=SLAB), :] = val[:, s * 128:(s + 1) * 128]


def _embed_kernel(x_ref, meta_ref, o_ref, *, chunk):
    seq, d = x_ref.shape[1], x_ref.shape[2]
    _store_tokens(o_ref, jnp.zeros((PAD, d), F32))
    _store_tokens(o_ref, meta_ref[...], first=PAD)
    for c in range(seq // chunk):
        _store_tokens(o_ref, x_ref[0, c * chunk:(c + 1) * chunk, :], first=PAD + N_META + c * chunk)


def _embed(x, meta):
    bsz, seq, d = x.shape
    lp = PAD + N_META + seq
    return pl.pallas_call(
        functools.partial(_embed_kernel, chunk=_row_tile(seq, 512)),
        grid=(bsz,),
        in_specs=[pl.BlockSpec((1, seq, d), lambda b: (b, 0, 0)),
                  pl.BlockSpec(meta.shape, lambda b: (0, 0))],
        out_specs=pl.BlockSpec((lp * SLAB, 128), lambda b: (b, 0)),
        out_shape=jax.ShapeDtypeStruct((bsz * lp * SLAB, 128), F32),
        compiler_params=_params(("parallel",)),
        name="embed",
    )(x, meta)


def _split2(x):
    hi = x.astype(BF16)
    lo = (x - hi.astype(F32)).astype(BF16)
    return hi, lo


def _split3(x):
    hi = x.astype(BF16)
    r = x - hi.astype(F32)
    mid = r.astype(BF16)
    lo = (r - mid.astype(F32)).astype(BF16)
    return hi, mid, lo


def _in_proj_kernel(h_ref, g_ref, w_ref, cos_ref, sina_ref, sinb_ref,
                    u_ref, hq_ref, hf_ref, hi_ref, hg_ref, q_ref, k1_ref, k2_ref, v_ref):
    x = _load_tokens(h_ref, u_ref.shape[0])
    ms = jnp.mean(x * x, axis=-1, keepdims=True)
    xn = (x * lax.rsqrt(ms + RMS_EPS) * g_ref[...]).astype(BF16)

    def sec(lo, hi):
        return _dot(xn, w_ref[:, lo:hi])

    u_ref[...] = sec(0, 256)
    hq_ref[0] = sec(256, 512)
    hf_ref[0] = sec(512, 768)
    hi_ref[0] = sec(768, 1024)
    hg_ref[0] = sec(1024, 1280)

    cos = cos_ref[...]
    sina = sina_ref[...]
    sinb = sinb_ref[...]
    lane = lax.broadcasted_iota(jnp.int32, cos.shape, 1)
    first = lane < 64

    def rope(t):
        return t * cos + pltpu.roll(t, 96, 1) * sina + pltpu.roll(t, 32, 1) * sinb

    for pair in range(ATTN_HEADS // 2):
        qq = sec(1280 + pair * 256, 1536 + pair * 256)
        kk = sec(1792 + pair * 256, 2048 + pair * 256)
        for sub in range(2):
            cs = slice((2 * pair + sub) * 128, (2 * pair + sub + 1) * 128)
            q = rope(qq[:, sub * 128:(sub + 1) * 128]) * Q_SCALE
            q_ref[0, :, cs] = q.astype(BF16)
            k = rope(kk[:, sub * 128:(sub + 1) * 128])
            k1_ref[0, :, cs] = jnp.where(first, k, 0.0).astype(BF16)
            k2_ref[0, :, cs] = jnp.where(first, 0.0, k).astype(BF16)
    v_ref[0, :, 0:256] = sec(2304, 2560).astype(BF16)
    v_ref[0, :, 256:512] = sec(2560, 2816).astype(BF16)


def _in_proj(h_slab, bsz, lp, g, w_bf, cos, sina, sinb):
    d = SLAB * 128
    tl = _row_tile(lp, 544)
    nl = lp // tl
    row = lambda b, i: (b, i, 0)
    tab = pl.BlockSpec((tl, 128), lambda b, i: (i, 0))
    out_shape = (
        jax.ShapeDtypeStruct((lp, bsz * 256), F32),
        jax.ShapeDtypeStruct((bsz, lp, 256), F32),
        jax.ShapeDtypeStruct((bsz, lp, 256), F32),
        jax.ShapeDtypeStruct((bsz, lp, 256), F32),
        jax.ShapeDtypeStruct((bsz, lp, 256), F32),
        jax.ShapeDtypeStruct((bsz, lp, 512), BF16),
        jax.ShapeDtypeStruct((bsz, lp, 512), BF16),
        jax.ShapeDtypeStruct((bsz, lp, 512), BF16),
        jax.ShapeDtypeStruct((bsz, lp, 512), BF16),
    )
    s256 = pl.BlockSpec((1, tl, 256), row)
    s512 = pl.BlockSpec((1, tl, 512), row)
    return pl.pallas_call(
        _in_proj_kernel,
        grid=(bsz, nl),
        in_specs=[pl.BlockSpec((tl * SLAB, 128), lambda b, i: (b * nl + i, 0)),
                  pl.BlockSpec((1, d), lambda b, i: (0, 0)),
                  pl.BlockSpec(w_bf.shape, lambda b, i: (0, 0)),
                  tab, tab, tab],
        out_specs=(pl.BlockSpec((tl, 256), lambda b, i: (i, b)), s256, s256, s256, s256,
                   s512, s512, s512, s512),
        out_shape=out_shape,
        compiler_params=_params(("parallel", "parallel")),
        name="in_proj",
    )(h_slab, g, w_bf, cos, sina, sinb)


def _s5_kernel(u_ref, bbar_ref, a_ref, cmat_ref, d_ref, wglu_ref, y_ref, xs_ref, st_ref):
    tt = u_ref.shape[0]
    nst = a_ref.shape[1] // 2

    @pl.when(pl.program_id(1) == 0)
    def _():
        st_ref[...] = jnp.zeros_like(st_ref)

    u2 = u_ref[...].reshape(tt * 8, u_ref.shape[2])
    xs_ref[...] = _dot(u2.astype(BF16), bbar_ref[...])

    ar = a_ref[:, :nst]
    ai = a_ref[:, nst:]

    def step(t, carry):
        sr, si = carry
        r0 = pl.multiple_of(t * 8, 8)
        xr = xs_ref[pl.ds(r0, 8), :nst]
        xi = xs_ref[pl.ds(r0, 8), nst:]
        nr = ar * sr - ai * si + xr
        ni = ar * si + ai * sr + xi
        xs_ref[pl.ds(r0, 8), :nst] = nr
        xs_ref[pl.ds(r0, 8), nst:] = ni
        return nr, ni

    sr, si = lax.fori_loop(0, tt, step, (st_ref[:, :nst], st_ref[:, nst:]))
    st_ref[:, :nst] = sr
    st_ref[:, nst:] = si

    y = _dot(xs_ref[...].astype(BF16), cmat_ref[...]) + d_ref[...] * u2
    g = 0.5 * y * (1.0 + jnp.tanh(0.7978845608028654 * (y + 0.044715 * (y * y * y))))
    gl = _dot(g.astype(BF16), wglu_ref[...])
    w = gl.shape[1] // 2
    out = gl[:, :w] * _sigmoid(gl[:, w:])
    y_ref[...] = out.reshape(y_ref.shape)


def _s5(u3, bbar, a8, cmat, dskip, wglu):
    lp, bsz, w = u3.shape
    tt = 128
    nst2 = a8.shape[1]
    const = lambda bg, i: (0, 0)
    return pl.pallas_call(
        _s5_kernel,
        grid=(bsz // 8, lp // tt),
        in_specs=[pl.BlockSpec((tt, 8, w), lambda bg, i: (i, bg, 0)),
                  pl.BlockSpec(bbar.shape, const),
                  pl.BlockSpec(a8.shape, const),
                  pl.BlockSpec(cmat.shape, const),
                  pl.BlockSpec(dskip.shape, const),
                  pl.BlockSpec(wglu.shape, const)],
        out_specs=pl.BlockSpec((tt, 8, w), lambda bg, i: (i, bg, 0)),
        out_shape=jax.ShapeDtypeStruct((lp, bsz, w), F32),
        scratch_shapes=[pltpu.VMEM((tt * 8, nst2), F32), pltpu.VMEM((8, nst2), F32)],
        compiler_params=_params(("parallel", "arbitrary")),
        name="s5_scan",
    )(u3, bbar, a8, cmat, dskip, wglu)


_LEVEL_HALVES = (32, 16, 8, 4, 2, 1)


def _hgrn_constants():
    c = HGRN_CHUNK
    t = np.arange(c)[:, None]
    j = np.arange(c)[None, :]
    mats = [(j <= t), (j > t)]
    for half in _LEVEL_HALVES:
        upper = (t & half) != 0
        seg_lo = t - (t % half)
        seg_hi = t | (half - 1)
        mats.append(np.where(upper, (j >= seg_lo) & (j <= t), (j > t) & (j <= seg_hi)))
    dmat = np.concatenate(mats, axis=0).astype(np.float32)
    s = np.arange(4 * c)[None, :] % c
    x = t ^ s
    hb = np.floor(np.log2(np.maximum(x, 1))).astype(np.int32)
    lv = np.where(s > t, -1, np.where(s == t, 6, hb)).astype(np.int32)
    hm = (np.arange(4 * c)[:, None] // c == np.arange(4 * c)[None, :] // c).astype(np.float32)
    return dmat, lv, hm


def _hgrn_kernel(zq_ref, zf_ref, zi_ref, zg_ref, lb_ref, ng_ref, dmat_ref, lv_ref, hm_ref,
                 o_ref, st_ref):
    c = HGRN_CHUNK
    w = zq_ref.shape[2]
    nchunks = zq_ref.shape[1] // c

    lb = lb_ref[...]
    lbm = jnp.maximum(lb, LB_FLOOR)
    one_m_lb = 1.0 - lb
    k_off = lb - lbm
    ng = ng_ref[...]
    row = lax.broadcasted_iota(jnp.int32, (c, w), 0)

    def chunks(ci, sts):
        seqs = range(len(sts))
        r0 = pl.multiple_of(ci * c, c)
        hm = hm_ref[...]
        hmb = hm.astype(BF16)
        lv = lv_ref[...]
        dmat = dmat_ref[...]

        zq = [zq_ref[bi, pl.ds(r0, c), :] for bi in seqs]
        zf = [zf_ref[bi, pl.ds(r0, c), :] for bi in seqs]
        vb = [zi_ref[bi, pl.ds(r0, c), :].astype(BF16) for bi in seqs]
        q = [z * _sigmoid(z) for z in zq]
        k = [one_m_lb * _sigmoid(-z) + k_off for z in zf]
        parts = [_split3(jnp.log(lbm + one_m_lb * _sigmoid(z))) for z in zf]
        e_all = [jnp.exp(_dot(dmat, hi) + _dot(dmat, mid) + _dot(dmat, lo)) for hi, mid, lo in parts]

        p = [jnp.zeros((c, 4 * c), F32) for _ in seqs]
        for li, half in enumerate(_LEVEL_HALVES):
            upper = (row & half) != 0
            s_l = []
            for bi in seqs:
                e_l = e_all[bi][(2 + li) * c:(3 + li) * c]
                q_l = jnp.where(upper, q[bi] * e_l, 0.0).astype(BF16)
                k_l = jnp.where(upper, 0.0, k[bi] * e_l).astype(BF16)
                s_l.append(_dot_nt(q_l, jnp.concatenate([k_l] * HGRN_HEADS, axis=0) * hmb))
            p = [jnp.where(lv == int(math.log2(half)), s_l[bi], p[bi]) for bi in seqs]
        s_d = [_dot((q[bi] * k[bi]).astype(BF16), hmb) for bi in seqs]
        p = [jnp.where(lv == 6, s_d[bi], p[bi]) for bi in seqs]

        o = [_dot(p[bi].astype(BF16), jnp.concatenate([vb[bi]] * HGRN_HEADS, axis=0) * hmb)
             + _dot_nt((q[bi] * e_all[bi][0:c]).astype(BF16), sts[bi].astype(BF16)) for bi in seqs]
        upd = [_dot_tn(vb[bi], (k[bi] * e_all[bi][c:2 * c]).astype(BF16)) for bi in seqs]
        new_sts = tuple(sts[bi] * e_all[bi][c - 1:c, :] + upd[bi] * hm for bi in seqs)

        for bi in seqs:
            oo_hi, oo_lo = _split2(o[bi] * o[bi])
            ms = (_dot(oo_hi, hmb) + _dot(oo_lo, hmb)) * (1.0 / (w // HGRN_HEADS))
            zg = zg_ref[bi, pl.ds(r0, c), :]
            o_ref[bi, pl.ds(r0, c), :] = o[bi] * lax.rsqrt(ms + RMS_EPS) * ng * (zg * _sigmoid(zg))
        return new_sts

    @pl.when(pl.program_id(1) == 0)
    def _():
        st_ref[...] = jnp.zeros_like(st_ref)

    sts = lax.fori_loop(0, nchunks, chunks, tuple(st_ref[bi] for bi in range(zq_ref.shape[0])))
    for bi, st in enumerate(sts):
        st_ref[bi] = st


def _hgrn(zq, zf, zi, zg, lb, ng):
    bsz, lp, w = zq.shape
    dmat, lv, hm = _hgrn_constants()
    dmat = jnp.asarray(dmat, BF16)
    lv = jnp.asarray(lv)
    hm = jnp.asarray(hm, F32)
    nb = HGRN_SEQS_PER_STEP if bsz % HGRN_SEQS_PER_STEP == 0 else 1
    nt = 2 if lp % (2 * HGRN_CHUNK) == 0 else 1
    seq = pl.BlockSpec((nb, lp // nt, w), lambda b, t: (b, t, 0))
    const = lambda b, t: (0, 0)
    return pl.pallas_call(
        _hgrn_kernel,
        grid=(bsz // nb, nt),
        in_specs=[seq, seq, seq, seq,
                  pl.BlockSpec(lb.shape, const), pl.BlockSpec(ng.shape, const),
                  pl.BlockSpec(dmat.shape, const), pl.BlockSpec(lv.shape, const),
                  pl.BlockSpec(hm.shape, const)],
        out_specs=seq,
        out_shape=jax.ShapeDtypeStruct((bsz, lp, w), F32),
        scratch_shapes=[pltpu.VMEM((nb, w, w), F32)],
        compiler_params=_params(("parallel", "arbitrary")),
        name="hgrn2",
    )(zq, zf, zi, zg, lb, ng, dmat, lv, hm)


ATTN_HEADS_PER_PASS = 1
ATTN_CHAINS = 2 * ATTN_HEADS_PER_PASS


def _attn_kernel(q_ref, k1_ref, k2_ref, v_ref, lam_ref, g_ref, o_ref, *scratch, tq, tk, out_scale):
    state = tuple(scratch[3 * n:3 * n + 3] for n in range(ATTN_CHAINS))
    s_a = scratch[3 * ATTN_CHAINS:4 * ATTN_CHAINS]
    s_b = scratch[4 * ATTN_CHAINS:5 * ATTN_CHAINS]
    i = pl.program_id(1)
    lp = k1_ref.shape[1]
    q0 = i * tq
    nkv = (q0 + tq + tk - 1) // tk
    kloc = lax.broadcasted_iota(jnp.int32, (tq, tk), 1)
    k_minus_q = kloc - lax.broadcasted_iota(jnp.int32, (tq, tk), 0)
    nrep = tk // 128
    lam = lam_ref[...]
    g = g_ref[...]

    def key_start(j):
        return pl.multiple_of(jnp.minimum(j * tk, lp - tk), 128)

    for h0 in range(0, ATTN_HEADS, ATTN_HEADS_PER_PASS):
        cols = [slice(hd * 128, (hd + 1) * 128) for hd in range(h0, h0 + ATTN_HEADS_PER_PASS)]
        chain_cols = [cols[n // 2] for n in range(ATTN_CHAINS)]
        chain_keys = [(k1_ref, k2_ref)[n % 2] for n in range(ATTN_CHAINS)]
        qh = [q_ref[0, :, cs] for cs in chain_cols]
        for m_ref, l_ref, acc_ref in state:
            m_ref[...] = jnp.full(m_ref.shape, MASK_VALUE, F32)
            l_ref[...] = jnp.zeros(l_ref.shape, F32)
            acc_ref[...] = jnp.zeros(acc_ref.shape, F32)

        def scores(j, into):
            k0 = key_start(j)
            for n in range(ATTN_CHAINS):
                into[n][...] = _dot_nt(qh[n], chain_keys[n][0, pl.ds(k0, tk), chain_cols[n]])

        def block(j, cur, nxt):
            if nxt is not None:
                scores(j + 1, nxt)
            k0 = key_start(j)
            valid = (k_minus_q <= q0 - k0) & (kloc >= jnp.maximum(PAD, j * tk) - k0)
            for n in range(ATTN_CHAINS):
                m_ref, l_ref, acc_ref = state[n]
                s = jnp.where(valid, cur[n][...], MASK_VALUE)
                m_prev = m_ref[...]
                m_next = jnp.maximum(m_prev, jnp.max(s, axis=-1, keepdims=True))
                alpha = jnp.exp2(m_prev - m_next)
                p = jnp.exp2(s - jnp.concatenate([m_next] * nrep, axis=1))
                psum = p[:, 0:128]
                for r in range(1, nrep):
                    psum = psum + p[:, r * 128:(r + 1) * 128]
                m_ref[...] = m_next
                l_ref[...] = alpha * l_ref[...] + psum
                vb = v_ref[0, pl.ds(k0, tk), chain_cols[n]]
                acc_ref[...] = alpha * acc_ref[...] + _dot(p.astype(BF16), vb)

        def pair(t, carry):
            block(2 * t, s_a, s_b)
            block(2 * t + 1, s_b, s_a)
            return carry

        scores(0, s_a)
        lax.fori_loop(0, nkv // 2, pair, 0)

        @pl.when(nkv % 2 == 1)
        def _():
            block(nkv - 1, s_a, None)

        for hh, cs in enumerate(cols):
            (_, l1_ref, acc1_ref), (_, l2_ref, acc2_ref) = state[2 * hh], state[2 * hh + 1]
            l1 = jnp.sum(l1_ref[...], axis=-1, keepdims=True)
            l2 = jnp.sum(l2_ref[...], axis=-1, keepdims=True)
            o = acc1_ref[...] / l1 - lam * (acc2_ref[...] / l2)
            ms = jnp.mean(o * o, axis=-1, keepdims=True)
            o_ref[0, :, cs] = (o * lax.rsqrt(ms + RMS_EPS) * g * out_scale).astype(o_ref.dtype)


def _attn(q, k1, k2, v, lam_row, g_row, out_scale):
    bsz, lp, w = q.shape
    tq = _row_tile(lp, 544)
    tk = 256
    qs = pl.BlockSpec((1, tq, w), lambda b, i: (b, i, 0))
    full = pl.BlockSpec((1, lp, w), lambda b, i: (b, 0, 0))
    row = pl.BlockSpec((1, 128), lambda b, i: (0, 0))
    return pl.pallas_call(
        functools.partial(_attn_kernel, tq=tq, tk=tk, out_scale=out_scale),
        grid=(bsz, lp // tq),
        in_specs=[qs, full, full, full, row, row],
        out_specs=qs,
        out_shape=jax.ShapeDtypeStruct((bsz, lp, w), BF16),
        scratch_shapes=([pltpu.VMEM((tq, 128), F32)] * (3 * ATTN_CHAINS)
                        + [pltpu.VMEM((tq, tk), F32)] * (2 * ATTN_CHAINS)),
        compiler_params=_params(("parallel", "arbitrary")),
        name="diff_attn",
    )(q, k1, k2, v, lam_row, g_row)


def _router_logits(hn, wr_ref, br_ref):
    x_hi, x_lo = _split2(hn)
    both = _dot(x_hi, wr_ref[...])
    return (both[:, :ROUTER_LANES] + both[:, ROUTER_LANES:] + _dot(x_lo, wr_ref[:, :ROUTER_LANES])
            + br_ref[...])


def _router_group(logits):
    lane = lax.broadcasted_iota(jnp.int32, logits.shape, 1)
    is_g = lane < MOE_GROUPS
    glog = jnp.where(is_g, logits, MASK_VALUE)
    gmax = jnp.max(glog, axis=-1, keepdims=True)
    gsum = jnp.sum(jnp.where(is_g, jnp.exp(glog - gmax), 0.0), axis=-1, keepdims=True)
    gi = jnp.min(jnp.where(is_g & (glog == gmax), lane.astype(F32), 1e9), axis=-1, keepdims=True)
    return gi, 1.0 / gsum


def _router_gates(logits, gi, gp):
    lane = lax.broadcasted_iota(jnp.int32, logits.shape, 1)
    lanef = lane.astype(F32)
    big = 1e9
    e_idx = lane - MOE_GROUPS
    in_group = (e_idx >= 0) & (e_idx < MOE_EXPERTS) & ((e_idx >> 2) == gi)
    elog = jnp.where(in_group, logits, MASK_VALUE)
    v1 = jnp.max(elog, axis=-1, keepdims=True)
    i1 = jnp.min(jnp.where(in_group & (elog == v1), lanef, big), axis=-1, keepdims=True)
    rest = in_group & (lanef != i1)
    elog2 = jnp.where(rest, logits, MASK_VALUE)
    v2 = jnp.max(elog2, axis=-1, keepdims=True)
    i2 = jnp.min(jnp.where(rest & (elog2 == v2), lanef, big), axis=-1, keepdims=True)
    e21 = jnp.exp(v2 - v1)
    den = 1.0 / (1.0 + e21)
    return jnp.where(lanef == i1, gp * den, 0.0) + jnp.where(lanef == i2, gp * e21 * den, 0.0)


def _out_proj_kernel(h_ref, ys_ref, yr_ref, ya_ref, wo_ref, gn_ref, wr_ref, br_ref,
                     hout_ref, gi_ref, *, tl):
    acc = _dot(ys_ref[...].astype(BF16), wo_ref[0:256, :])
    acc += _dot(yr_ref[0].astype(BF16), wo_ref[256:512, :])
    acc += _dot(ya_ref[0], wo_ref[512:1024, :])
    h = _load_tokens(h_ref, tl) + acc
    rowpos = pl.program_id(1) * tl + lax.broadcasted_iota(jnp.int32, h.shape, 0)
    h = jnp.where(rowpos >= PAD, h, 0.0)
    _store_tokens(hout_ref, h)
    ms = jnp.mean(h * h, axis=-1, keepdims=True)
    hn = h * lax.rsqrt(ms + RMS_EPS) * gn_ref[...]
    gi, _ = _router_group(_router_logits(hn, wr_ref, br_ref))
    gi_ref[0] = gi.astype(jnp.int32)


def _out_proj(h_slab, bsz, lp, ys2d, yr, ya, wo_bf, gn, wr, br):
    d = SLAB * 128
    tl = _row_tile(lp, 544)
    nl = lp // tl
    row = lambda b, i: (b, i, 0)
    const = lambda b, i: (0, 0)
    slab = pl.BlockSpec((tl * SLAB, 128), lambda b, i: (b * nl + i, 0))
    return pl.pallas_call(
        functools.partial(_out_proj_kernel, tl=tl),
        grid=(bsz, nl),
        in_specs=[slab,
                  pl.BlockSpec((tl, 256), lambda b, i: (i, b)),
                  pl.BlockSpec((1, tl, 256), row),
                  pl.BlockSpec((1, tl, 512), row),
                  pl.BlockSpec(wo_bf.shape, const),
                  pl.BlockSpec(gn.shape, const),
                  pl.BlockSpec(wr.shape, const),
                  pl.BlockSpec(br.shape, const)],
        out_specs=(slab, pl.BlockSpec((1, tl, 1), row)),
        out_shape=(jax.ShapeDtypeStruct((bsz * lp * SLAB, 128), F32),
                   jax.ShapeDtypeStruct((bsz, lp, 1), jnp.int32)),
        compiler_params=_params(("parallel", "parallel")),
        name="out_proj_router",
    )(h_slab, ys2d, yr, ya, wo_bf, gn, wr, br)


def _moe_kernel(grp_ref, used_ref, src_ref, srcn_ref, dst_ref, h_hbm, gn_ref, wr_ref, br_ref,
                wg_ref, wu_ref, wd_ref, out_hbm, xbuf, ybuf, gsem, ssem, *, tm):
    p = pl.program_id(0)
    used = used_ref[0]
    g = grp_ref[p]
    slot = p % 2

    def for_rows(start_row):
        def body(i, carry):
            for u in range(ROW_UNROLL):
                start_row(i * ROW_UNROLL + u, u % 2)
            return carry
        lax.fori_loop(0, tm // ROW_UNROLL, body, 0)

    def gather(idx_ref, into):
        def start_row(r, prio):
            rows = pl.ds(pl.multiple_of((into * tm + r) * SLAB, SLAB), SLAB)
            pltpu.make_async_copy(h_hbm.at[idx_ref[0, 0, r]], xbuf.at[rows, :],
                                  gsem.at[into]).start(priority=prio)
        for_rows(start_row)

    def scatter(dst_of_row):
        def start_row(r, prio):
            rows = pl.ds(pl.multiple_of(r * SLAB, SLAB), SLAB)
            pltpu.make_async_copy(ybuf.at[rows, :], out_hbm.at[dst_of_row(r)],
                                  ssem.at[0]).start(priority=prio)
        for_rows(start_row)

    def wait_tile(sem):
        pltpu.make_async_copy(h_hbm.at[pl.ds(0, tm)], out_hbm.at[pl.ds(0, tm)], sem).wait()

    @pl.when(p == 0)
    def _():
        gather(src_ref, 0)
        spare = out_hbm.shape[0] - tm
        ybuf[...] = jnp.zeros_like(ybuf)
        scatter(lambda r: spare + r)
        wait_tile(ssem.at[0])

    @pl.when(p < used)
    def _():
        wait_tile(gsem.at[slot])

        for s in range(2):
            @pl.when((p + 1 < used) & (slot == s))
            def _():
                gather(srcn_ref, 1 - s)

        h = _load_tokens(xbuf, tm, first=slot * tm)
        ms = jnp.mean(h * h, axis=-1, keepdims=True)
        hn = h * lax.rsqrt(ms + RMS_EPS) * gn_ref[...]
        logits = _router_logits(hn, wr_ref, br_ref)
        lane = lax.broadcasted_iota(jnp.int32, logits.shape, 1)
        is_g = lane < MOE_GROUPS
        glog = jnp.where(is_g, logits, MASK_VALUE)
        gmax = jnp.max(glog, axis=-1, keepdims=True)
        gsum = jnp.sum(jnp.where(is_g, jnp.exp(glog - gmax), 0.0), axis=-1, keepdims=True)
        lg = jnp.sum(jnp.where(lane == g, logits, 0.0), axis=-1, keepdims=True)
        gates = _router_gates(logits, g, jnp.exp(lg - gmax) / gsum)

        x = hn.astype(BF16)
        y = h
        for e in range(MOE_EXPERTS_PER_GROUP):
            ge = jnp.sum(jnp.where(lane == MOE_GROUPS + MOE_EXPERTS_PER_GROUP * g + e, gates, 0.0),
                         axis=-1, keepdims=True)
            a = _dot(x, wg_ref[e])
            b = _dot(x, wu_ref[e])
            he = (a * _sigmoid(a)) * b * ge
            y = y + _dot(he.astype(BF16), wd_ref[e])

        @pl.when(p > 0)
        def _():
            wait_tile(ssem.at[0])

        _store_tokens(ybuf, y)
        scatter(lambda r: dst_ref[0, 0, r])

        @pl.when(p + 1 == used)
        def _():
            wait_tile(ssem.at[0])


def _moe_routed(h_slab, gi, gn, wr, br, wg, wu, wd, tm=512):
    t = gi.shape[0]
    ne, d, ff = wg.shape
    tm = _row_tile(t, tm)
    nt = t // tm + MOE_GROUPS

    order = jnp.argsort(gi, stable=True).astype(jnp.int32)
    counts = jnp.sum((gi[:, None] == jnp.arange(MOE_GROUPS, dtype=jnp.int32)[None, :]).astype(jnp.int32),
                     axis=0)
    starts = jnp.cumsum(counts) - counts
    tiles_g = (counts + tm - 1) // tm
    tile_end = jnp.cumsum(tiles_g)
    tile_start = tile_end - tiles_g
    pidx = jnp.arange(nt, dtype=jnp.int32)
    grp = jnp.minimum(jnp.sum((pidx[:, None] >= tile_end[None, :]).astype(jnp.int32), axis=1),
                      MOE_GROUPS - 1)
    j = pidx - tile_start[grp]
    nval = jnp.where(pidx < tile_end[-1], jnp.clip(counts[grp] - j * tm, 0, tm), 0)
    lane = jnp.arange(tm, dtype=jnp.int32)[None, :]
    rows = starts[grp][:, None] + j[:, None] * tm + lane
    src = order[jnp.clip(rows, 0, t - 1)]
    dst = jnp.where(lane < nval[:, None], src, t + lane)
    src = src.reshape(nt, 1, tm)
    dst = dst.reshape(nt, 1, tm).astype(jnp.int32)
    used = tile_end[-1:].astype(jnp.int32)

    const2 = lambda p, grp, used: (0, 0)
    wspec = lambda shape: pl.BlockSpec((MOE_EXPERTS_PER_GROUP,) + shape, lambda p, grp, used: (grp[p], 0, 0))
    idx_spec = lambda shift: pl.BlockSpec(
        (1, 1, tm), lambda p, grp, used: (jnp.minimum(p + shift, nt - 1), 0, 0), memory_space=pltpu.SMEM)
    grid_spec = pltpu.PrefetchScalarGridSpec(
        num_scalar_prefetch=2,
        grid=(nt,),
        in_specs=[idx_spec(0), idx_spec(1), idx_spec(0),
                  pl.BlockSpec(memory_space=pl.ANY),
                  pl.BlockSpec(gn.shape, const2),
                  pl.BlockSpec(wr.shape, const2),
                  pl.BlockSpec(br.shape, const2),
                  wspec((d, ff)), wspec((d, ff)), wspec((ff, d))],
        out_specs=pl.BlockSpec(memory_space=pl.ANY),
        scratch_shapes=[pltpu.VMEM((2 * tm * SLAB, 128), F32), pltpu.VMEM((tm * SLAB, 128), F32),
                        pltpu.SemaphoreType.DMA((2,)), pltpu.SemaphoreType.DMA((1,))],
    )
    n_in = h_slab.shape[0] // SLAB
    out = pl.pallas_call(
        functools.partial(_moe_kernel, tm=tm),
        grid_spec=grid_spec,
        out_shape=jax.ShapeDtypeStruct((t + tm, SLAB, 128), F32),
        compiler_params=_params(("arbitrary",)),
        name="moe_experts",
    )(grp, used, src, src, dst, h_slab.reshape(n_in, SLAB, 128), gn, wr, br, wg, wu, wd)
    return out.reshape((t + tm) * SLAB, 128)


def _final_norm_kernel(h_ref, g_ref, o_ref, *, skip, chunk):
    g = g_ref[...]
    for c in range(o_ref.shape[1] // chunk):
        x = _load_tokens(h_ref, chunk, first=skip + c * chunk)
        ms = jnp.mean(x * x, axis=-1, keepdims=True)
        o_ref[0, c * chunk:(c + 1) * chunk, :] = x * lax.rsqrt(ms + RMS_EPS) * g


def _final_norm(h_slab, bsz, lp, g, seq):
    d = SLAB * 128
    chunk = _row_tile(seq, 512)
    return pl.pallas_call(
        functools.partial(_final_norm_kernel, skip=lp - seq, chunk=chunk),
        grid=(bsz,),
        in_specs=[pl.BlockSpec((lp * SLAB, 128), lambda b: (b, 0)),
                  pl.BlockSpec((1, d), lambda b: (0, 0))],
        out_specs=pl.BlockSpec((1, seq, d), lambda b: (b, 0, 0)),
        out_shape=jax.ShapeDtypeStruct((bsz, seq, d), F32),
        compiler_params=_params(("parallel",)),
        name="final_norm",
    )(h_slab, g.reshape(1, d))


def _s5_tables(lam_re, lam_im, log_dt, b_re, b_im, c_re, c_im):
    ng, ns = lam_re.shape
    lr = lam_re.astype(F32)
    li = lam_im.astype(F32)
    dt = jnp.exp(log_dt.astype(F32))[:, None]
    mag = jnp.exp(lr * dt)
    abar_r = mag * jnp.cos(li * dt)
    abar_i = mag * jnp.sin(li * dt)
    den = lr * lr + li * li
    zr = abar_r - 1.0
    zi = abar_i
    fr = (zr * lr + zi * li) / den
    fi = (zi * lr - zr * li) / den
    br = b_re.astype(F32)
    bi = b_im.astype(F32)
    bb_r = fr[..., None] * br - fi[..., None] * bi
    bb_i = fr[..., None] * bi + fi[..., None] * br
    eye = jnp.eye(ng, dtype=F32)
    nch = b_re.shape[2]
    to_b = lambda m: jnp.einsum('gnc,gh->gchn', m, eye).reshape(ng * nch, ng * ns)
    bbar = jnp.concatenate([to_b(bb_r), to_b(bb_i)], axis=1).astype(BF16)
    to_c = lambda m: jnp.einsum('gcn,gh->gnhc', m.astype(F32), eye).reshape(ng * ns, ng * nch)
    cmat = jnp.concatenate([to_c(c_re), -to_c(c_im)], axis=0).astype(BF16)
    a_row = jnp.concatenate([abar_r.reshape(1, -1), abar_i.reshape(1, -1)], axis=1)
    a8 = jnp.broadcast_to(a_row, (8, a_row.shape[1]))
    return bbar, a8, cmat


def _rope_tables(lp):
    half = 32
    inv_freq = 1.0 / (ROPE_THETA ** (jnp.arange(0, 2 * half, 2, dtype=F32) / (2 * half)))
    pos = jnp.arange(lp, dtype=F32) - float(PAD)
    ang = pos[:, None] * inv_freq[None, :]
    ang = jnp.concatenate([ang, ang, ang, ang], axis=-1)
    cos = jnp.cos(ang)
    sin = jnp.sin(ang)
    first = (jnp.arange(128) % 64) < half
    sina = jnp.where(first[None, :], -sin, 0.0)
    sinb = jnp.where(first[None, :], 0.0, sin)
    return cos, sina, sinb


def kernel(x, meta_tokens, norm_mix_g, w_in, s5_lambda_re, s5_lambda_im, s5_log_dt, s5_b_re, s5_b_im, s5_c_re, s5_c_im, s5_d, s5_w_glu, hgrn_lower_bounds, hgrn_norm_g, diff_lambda_q1, diff_lambda_k1, diff_lambda_q2, diff_lambda_k2, diff_subln_g, w_out, norm_ffn_g, moe_w_group, moe_b_group, moe_w_expert, moe_b_expert, moe_w_gate, moe_w_up, moe_w_down, final_norm_g):
    bsz, seq, d = x.shape
    depth = w_in.shape[0]
    lp = PAD + N_META + seq
    assert lp % 128 == 0 and bsz % 8 == 0

    assert d == SLAB * 128 and meta_tokens.shape[0] == N_META
    h = _embed(x.astype(F32), meta_tokens.astype(F32))

    cos, sina, sinb = _rope_tables(lp)
    lb_w = jax.nn.softmax(hgrn_lower_bounds.astype(F32), axis=0)
    lower_bounds = jnp.cumsum(lb_w, axis=0) - lb_w[0:1]

    for layer in range(depth):
        lam_init = 0.8 - 0.6 * math.exp(-0.3 * layer)
        u2d, hq, hf, hi, hg, aq, ak1, ak2, av = _in_proj(
            h, bsz, lp, norm_mix_g[layer].reshape(1, d).astype(F32), w_in[layer].astype(BF16), cos, sina, sinb)

        bbar, a8, cmat = _s5_tables(s5_lambda_re[layer], s5_lambda_im[layer], s5_log_dt[layer],
                                    s5_b_re[layer], s5_b_im[layer], s5_c_re[layer], s5_c_im[layer])
        y_ssm = _s5(u2d.reshape(lp, bsz, 256), bbar, a8, cmat,
                    s5_d[layer].reshape(1, -1).astype(F32), s5_w_glu[layer].astype(BF16))

        ng = jnp.tile(hgrn_norm_g[layer].astype(F32), HGRN_HEADS).reshape(1, -1)
        y_rec = _hgrn(hq, hf, hi, hg, lower_bounds[layer].reshape(1, -1), ng)

        lam = (jnp.exp(jnp.sum(diff_lambda_q1[layer].astype(F32) * diff_lambda_k1[layer].astype(F32)))
               - jnp.exp(jnp.sum(diff_lambda_q2[layer].astype(F32) * diff_lambda_k2[layer].astype(F32)))
               + lam_init)
        y_att = _attn(aq, ak1, ak2, av, jnp.full((1, 128), lam, F32),
                      diff_subln_g[layer].reshape(1, -1).astype(F32), 1.0 - lam_init)

        wr = jnp.concatenate([moe_w_group[layer].astype(F32), moe_w_expert[layer].astype(F32)], axis=1)
        wr = jnp.pad(wr, ((0, 0), (0, ROUTER_LANES - wr.shape[1])))
        wr_hi = wr.astype(BF16)
        wr = jnp.concatenate([wr_hi, (wr - wr_hi.astype(F32)).astype(BF16)], axis=1)
        br = jnp.concatenate([moe_b_group[layer].astype(F32), moe_b_expert[layer].astype(F32)])
        br = jnp.pad(br, (0, ROUTER_LANES - br.shape[0])).reshape(1, -1)
        gn = norm_ffn_g[layer].reshape(1, d).astype(F32)
        h, gi = _out_proj(h, bsz, lp, y_ssm.reshape(lp, bsz * 256), y_rec, y_att,
                          w_out[layer].astype(BF16), gn, wr, br)
        h = _moe_routed(h, gi.reshape(bsz * lp), gn, wr, br, moe_w_gate[layer].astype(BF16),
                        moe_w_up[layer].astype(BF16), moe_w_down[layer].astype(BF16))

    return _final_norm(h, bsz, lp, final_norm_g.astype(F32), seq)
```

```python
import functools
import math

import numpy as np
import jax
import jax.numpy as jnp
from jax import lax
from jax.experimental import pallas as pl
from jax.experimental.pallas import tpu as pltpu

F32 = jnp.float32
BF16 = jnp.bfloat16

N_META = 16
PAD = 112
RMS_EPS = 1e-6
MASK_VALUE = -1e30
LB_FLOOR = 1e-30
ROPE_THETA = 10000.0

S5_GROUP_CH = 16
S5_STATE = 64
HGRN_HEADS = 4
HGRN_CHUNK = 64
HGRN_SEQS_PER_STEP = 4
ATTN_HEADS = 4
ATTN_QK_DIM = 64
Q_SCALE = ATTN_QK_DIM ** -0.5 * math.log2(math.e)
MOE_GROUPS = 4
MOE_EXPERTS_PER_GROUP = 4
MOE_EXPERTS = MOE_GROUPS * MOE_EXPERTS_PER_GROUP
ROUTER_LANES = 128
ROW_UNROLL = 8

VMEM_LIMIT = 56 * 1024 * 1024


def _params(sem, vmem=VMEM_LIMIT):
    return pltpu.CompilerParams(dimension_semantics=sem, vmem_limit_bytes=vmem)


def _row_tile(n, cap, mult=16):
    best = None
    for t in range(mult, min(n, cap) + 1, mult):
        if n % t == 0:
            best = t
    assert best is not None, (n, cap)
    return best


def _sigmoid(x):
    return 1.0 / (1.0 + jnp.exp(-x))


def _dot(a, b):
    return jnp.dot(a, b, preferred_element_type=F32)


def _dot_nt(a, b):
    return lax.dot_general(a, b, (((1,), (1,)), ((), ())), preferred_element_type=F32)


def _dot_tn(a, b):
    return lax.dot_general(a, b, (((0,), (0,)), ((), ())), preferred_element_type=F32)


SLAB = 8


def _load_tokens(ref, n, first=0):
    return jnp.concatenate([ref[pl.ds(first * SLAB + s, n, stride=SLAB), :] for s in range(SLAB)], axis=1)


def _store_tokens(ref, val, first=0):
    n = val.shape[0]
    for s in range(SLAB):
        ref[pl.ds(first * SLAB + s, n, stride=SLAB), :] = val[:, s * 128:(s + 1) * 128]


def _embed_kernel(x_ref, meta_ref, o_ref, *, chunk):
    seq, d = x_ref.shape[1], x_ref.shape[2]
    _store_tokens(o_ref, jnp.zeros((PAD, d), F32))
    _store_tokens(o_ref, meta_ref[...], first=PAD)
    for c in range(seq // chunk):
        _store_tokens(o_ref, x_ref[0, c * chunk:(c + 1) * chunk, :], first=PAD + N_META + c * chunk)


def _embed(x, meta):
    bsz, seq, d = x.shape
    lp = PAD + N_META + seq
    return pl.pallas_call(
        functools.partial(_embed_kernel, chunk=_row_tile(seq, 512)),
        grid=(bsz,),
        in_specs=[pl.BlockSpec((1, seq, d), lambda b: (b, 0, 0)),
                  pl.BlockSpec(meta.shape, lambda b: (0, 0))],
        out_specs=pl.BlockSpec((lp * SLAB, 128), lambda b: (b, 0)),
        out_shape=jax.ShapeDtypeStruct((bsz * lp * SLAB, 128), F32),
        compiler_params=_params(("parallel",)),
        name="embed",
    )(x, meta)


def _split2(x):
    hi = x.astype(BF16)
    lo = (x - hi.astype(F32)).astype(BF16)
    return hi, lo


def _split3(x):
    hi = x.astype(BF16)
    r = x - hi.astype(F32)
    mid = r.astype(BF16)
    lo = (r - mid.astype(F32)).astype(BF16)
    return hi, mid, lo


def _in_proj_kernel(h_ref, g_ref, w_ref, cos_ref, sina_ref, sinb_ref,
                    u_ref, hq_ref, hf_ref, hi_ref, hg_ref, q_ref, k1_ref, k2_ref, v_ref):
    x = _load_tokens(h_ref, u_ref.shape[0])
    ms = jnp.mean(x * x, axis=-1, keepdims=True)
    xn = (x * lax.rsqrt(ms + RMS_EPS) * g_ref[...]).astype(BF16)

    def sec(lo, hi):
        return _dot(xn, w_ref[:, lo:hi])

    u_ref[...] = sec(0, 256)
    hq_ref[0] = sec(256, 512)
    hf_ref[0] = sec(512, 768)
    hi_ref[0] = sec(768, 1024)
    hg_ref[0] = sec(1024, 1280)

    cos = cos_ref[...]
    sina = sina_ref[...]
    sinb = sinb_ref[...]
    lane = lax.broadcasted_iota(jnp.int32, cos.shape, 1)
    first = lane < 64

    def rope(t):
        return t * cos + pltpu.roll(t, 96, 1) * sina + pltpu.roll(t, 32, 1) * sinb

    for pair in range(ATTN_HEADS // 2):
        qq = sec(1280 + pair * 256, 1536 + pair * 256)
        kk = sec(1792 + pair * 256, 2048 + pair * 256)
        for sub in range(2):
            cs = slice((2 * pair + sub) * 128, (2 * pair + sub + 1) * 128)
            q = rope(qq[:, sub * 128:(sub + 1) * 128]) * Q_SCALE
            q_ref[0, :, cs] = q.astype(BF16)
            k = rope(kk[:, sub * 128:(sub + 1) * 128])
            k1_ref[0, :, cs] = jnp.where(first, k, 0.0).astype(BF16)
            k2_ref[0, :, cs] = jnp.where(first, 0.0, k).astype(BF16)
    v_ref[0, :, 0:256] = sec(2304, 2560).astype(BF16)
    v_ref[0, :, 256:512] = sec(2560, 2816).astype(BF16)


def _in_proj(h_slab, bsz, lp, g, w_bf, cos, sina, sinb):
    d = SLAB * 128
    tl = _row_tile(lp, 1088)
    nl = lp // tl
    row = lambda b, i: (b, i, 0)
    tab = pl.BlockSpec((tl, 128), lambda b, i: (i, 0))
    out_shape = (
        jax.ShapeDtypeStruct((lp, bsz * 256), F32),
        jax.ShapeDtypeStruct((bsz, lp, 256), F32),
        jax.ShapeDtypeStruct((bsz, lp, 256), F32),
        jax.ShapeDtypeStruct((bsz, lp, 256), F32),
        jax.ShapeDtypeStruct((bsz, lp, 256), F32),
        jax.ShapeDtypeStruct((bsz, lp, 512), BF16),
        jax.ShapeDtypeStruct((bsz, lp, 512), BF16),
        jax.ShapeDtypeStruct((bsz, lp, 512), BF16),
        jax.ShapeDtypeStruct((bsz, lp, 512), BF16),
    )
    s256 = pl.BlockSpec((1, tl, 256), row)
    s512 = pl.BlockSpec((1, tl, 512), row)
    return pl.pallas_call(
        _in_proj_kernel,
        grid=(bsz, nl),
        in_specs=[pl.BlockSpec((tl * SLAB, 128), lambda b, i: (b * nl + i, 0)),
                  pl.BlockSpec((1, d), lambda b, i: (0, 0)),
                  pl.BlockSpec(w_bf.shape, lambda b, i: (0, 0)),
                  tab, tab, tab],
        out_specs=(pl.BlockSpec((tl, 256), lambda b, i: (i, b)), s256, s256, s256, s256,
                   s512, s512, s512, s512),
        out_shape=out_shape,
        compiler_params=_params(("parallel", "parallel")),
        name="in_proj",
    )(h_slab, g, w_bf, cos, sina, sinb)


def _s5_kernel(u_ref, bbar_ref, a_ref, cmat_ref, d_ref, wglu_ref, y_ref, xs_ref, st_ref):
    tt = u_ref.shape[0]
    nst = a_ref.shape[1] // 2

    @pl.when(pl.program_id(1) == 0)
    def _():
        st_ref[...] = jnp.zeros_like(st_ref)

    u2 = u_ref[...].reshape(tt * 8, u_ref.shape[2])
    xs_ref[...] = _dot(u2.astype(BF16), bbar_ref[...])

    ar = a_ref[:, :nst]
    ai = a_ref[:, nst:]

    def step(t, carry):
        sr, si = carry
        r0 = pl.multiple_of(t * 8, 8)
        xr = xs_ref[pl.ds(r0, 8), :nst]
        xi = xs_ref[pl.ds(r0, 8), nst:]
        nr = ar * sr - ai * si + xr
        ni = ar * si + ai * sr + xi
        xs_ref[pl.ds(r0, 8), :nst] = nr
        xs_ref[pl.ds(r0, 8), nst:] = ni
        return nr, ni

    sr, si = lax.fori_loop(0, tt, step, (st_ref[:, :nst], st_ref[:, nst:]))
    st_ref[:, :nst] = sr
    st_ref[:, nst:] = si

    y = _dot(xs_ref[...].astype(BF16), cmat_ref[...]) + d_ref[...] * u2
    g = 0.5 * y * (1.0 + jnp.tanh(0.7978845608028654 * (y + 0.044715 * (y * y * y))))
    gl = _dot(g.astype(BF16), wglu_ref[...])
    w = gl.shape[1] // 2
    out = gl[:, :w] * _sigmoid(gl[:, w:])
    y_ref[...] = out.reshape(y_ref.shape)


def _s5(u3, bbar, a8, cmat, dskip, wglu):
    lp, bsz, w = u3.shape
    tt = 128
    nst2 = a8.shape[1]
    const = lambda bg, i: (0, 0)
    return pl.pallas_call(
        _s5_kernel,
        grid=(bsz // 8, lp // tt),
        in_specs=[pl.BlockSpec((tt, 8, w), lambda bg, i: (i, bg, 0)),
                  pl.BlockSpec(bbar.shape, const),
                  pl.BlockSpec(a8.shape, const),
                  pl.BlockSpec(cmat.shape, const),
                  pl.BlockSpec(dskip.shape, const),
                  pl.BlockSpec(wglu.shape, const)],
        out_specs=pl.BlockSpec((tt, 8, w), lambda bg, i: (i, bg, 0)),
        out_shape=jax.ShapeDtypeStruct((lp, bsz, w), F32),
        scratch_shapes=[pltpu.VMEM((tt * 8, nst2), F32), pltpu.VMEM((8, nst2), F32)],
        compiler_params=_params(("parallel", "arbitrary")),
        name="s5_scan",
    )(u3, bbar, a8, cmat, dskip, wglu)


_LEVEL_HALVES = (32, 16, 8, 4, 2, 1)


def _hgrn_constants():
    c = HGRN_CHUNK
    t = np.arange(c)[:, None]
    j = np.arange(c)[None, :]
    mats = [(j <= t), (j > t)]
    for half in _LEVEL_HALVES:
        upper = (t & half) != 0
        seg_lo = t - (t % half)
        seg_hi = t | (half - 1)
        mats.append(np.where(upper, (j >= seg_lo) & (j <= t), (j > t) & (j <= seg_hi)))
    dmat = np.concatenate(mats, axis=0).astype(np.float32)
    s = np.arange(4 * c)[None, :] % c
    x = t ^ s
    hb = np.floor(np.log2(np.maximum(x, 1))).astype(np.int32)
    lv = np.where(s > t, -1, np.where(s == t, 6, hb)).astype(np.int32)
    hm = (np.arange(4 * c)[:, None] // c == np.arange(4 * c)[None, :] // c).astype(np.float32)
    return dmat, lv, hm


def _hgrn_kernel(zq_ref, zf_ref, zi_ref, zg_ref, lb_ref, ng_ref, dmat_ref, lv_ref, hm_ref,
                 o_ref, st_ref):
    c = HGRN_CHUNK
    w = zq_ref.shape[2]
    nchunks = zq_ref.shape[1] // c

    lb = lb_ref[...]
    lbm = jnp.maximum(lb, LB_FLOOR)
    one_m_lb = 1.0 - lb
    k_off = lb - lbm
    ng = ng_ref[...]
    row = lax.broadcasted_iota(jnp.int32, (c, w), 0)

    def chunks(ci, sts):
        seqs = range(len(sts))
        r0 = pl.multiple_of(ci * c, c)
        hm = hm_ref[...]
        hmb = hm.astype(BF16)
        lv = lv_ref[...]
        dmat = dmat_ref[...]

        zq = [zq_ref[bi, pl.ds(r0, c), :] for bi in seqs]
        zf = [zf_ref[bi, pl.ds(r0, c), :] for bi in seqs]
        vb = [zi_ref[bi, pl.ds(r0, c), :].astype(BF16) for bi in seqs]
        q = [z * _sigmoid(z) for z in zq]
        k = [one_m_lb * _sigmoid(-z) + k_off for z in zf]
        parts = [_split3(jnp.log(lbm + one_m_lb * _sigmoid(z))) for z in zf]
        e_all = [jnp.exp(_dot(dmat, hi) + _dot(dmat, mid) + _dot(dmat, lo)) for hi, mid, lo in parts]

        p = [jnp.zeros((c, 4 * c), F32) for _ in seqs]
        for li, half in enumerate(_LEVEL_HALVES):
            upper = (row & half) != 0
            s_l = []
            for bi in seqs:
                e_l = e_all[bi][(2 + li) * c:(3 + li) * c]
                q_l = jnp.where(upper, q[bi] * e_l, 0.0).astype(BF16)
                k_l = jnp.where(upper, 0.0, k[bi] * e_l).astype(BF16)
                s_l.append(_dot_nt(q_l, jnp.concatenate([k_l] * HGRN_HEADS, axis=0) * hmb))
            p = [jnp.where(lv == int(math.log2(half)), s_l[bi], p[bi]) for bi in seqs]
        s_d = [_dot((q[bi] * k[bi]).astype(BF16), hmb) for bi in seqs]
        p = [jnp.where(lv == 6, s_d[bi], p[bi]) for bi in seqs]

        o = [_dot(p[bi].astype(BF16), jnp.concatenate([vb[bi]] * HGRN_HEADS, axis=0) * hmb)
             + _dot_nt((q[bi] * e_all[bi][0:c]).astype(BF16), sts[bi].astype(BF16)) for bi in seqs]
        upd = [_dot_tn(vb[bi], (k[bi] * e_all[bi][c:2 * c]).astype(BF16)) for bi in seqs]
        new_sts = tuple(sts[bi] * e_all[bi][c - 1:c, :] + upd[bi] * hm for bi in seqs)

        for bi in seqs:
            oo_hi, oo_lo = _split2(o[bi] * o[bi])
            ms = (_dot(oo_hi, hmb) + _dot(oo_lo, hmb)) * (1.0 / (w // HGRN_HEADS))
            zg = zg_ref[bi, pl.ds(r0, c), :]
            o_ref[bi, pl.ds(r0, c), :] = o[bi] * lax.rsqrt(ms + RMS_EPS) * ng * (zg * _sigmoid(zg))
        return new_sts

    @pl.when(pl.program_id(1) == 0)
    def _():
        st_ref[...] = jnp.zeros_like(st_ref)

    sts = lax.fori_loop(0, nchunks, chunks, tuple(st_ref[bi] for bi in range(zq_ref.shape[0])))
    for bi, st in enumerate(sts):
        st_ref[bi] = st


def _hgrn(zq, zf, zi, zg, lb, ng):
    bsz, lp, w = zq.shape
    dmat, lv, hm = _hgrn_constants()
    dmat = jnp.asarray(dmat, BF16)
    lv = jnp.asarray(lv)
    hm = jnp.asarray(hm, F32)
    nb = HGRN_SEQS_PER_STEP if bsz % HGRN_SEQS_PER_STEP == 0 else 1
    nt = 2 if lp % (2 * HGRN_CHUNK) == 0 else 1
    seq = pl.BlockSpec((nb, lp // nt, w), lambda b, t: (b, t, 0))
    const = lambda b, t: (0, 0)
    return pl.pallas_call(
        _hgrn_kernel,
        grid=(bsz // nb, nt),
        in_specs=[seq, seq, seq, seq,
                  pl.BlockSpec(lb.shape, const), pl.BlockSpec(ng.shape, const),
                  pl.BlockSpec(dmat.shape, const), pl.BlockSpec(lv.shape, const),
                  pl.BlockSpec(hm.shape, const)],
        out_specs=seq,
        out_shape=jax.ShapeDtypeStruct((bsz, lp, w), F32),
        scratch_shapes=[pltpu.VMEM((nb, w, w), F32)],
        compiler_params=_params(("parallel", "arbitrary")),
        name="hgrn2",
    )(zq, zf, zi, zg, lb, ng, dmat, lv, hm)


ATTN_HEADS_PER_PASS = 1
ATTN_CHAINS = 2 * ATTN_HEADS_PER_PASS


def _attn_kernel(q_ref, k1_ref, k2_ref, v_ref, lam_ref, g_ref, o_ref, *scratch, tq, tk, out_scale):
    state = tuple(scratch[3 * n:3 * n + 3] for n in range(ATTN_CHAINS))
    s_a = scratch[3 * ATTN_CHAINS:4 * ATTN_CHAINS]
    s_b = scratch[4 * ATTN_CHAINS:5 * ATTN_CHAINS]
    i = pl.program_id(1)
    lp = k1_ref.shape[1]
    q0 = i * tq
    nkv = (q0 + tq + tk - 1) // tk
    kloc = lax.broadcasted_iota(jnp.int32, (tq, tk), 1)
    k_minus_q = kloc - lax.broadcasted_iota(jnp.int32, (tq, tk), 0)
    nrep = tk // 128
    lam = lam_ref[...]
    g = g_ref[...]

    def key_start(j):
        return pl.multiple_of(jnp.minimum(j * tk, lp - tk), 128)

    for h0 in range(0, ATTN_HEADS, ATTN_HEADS_PER_PASS):
        cols = [slice(hd * 128, (hd + 1) * 128) for hd in range(h0, h0 + ATTN_HEADS_PER_PASS)]
        chain_cols = [cols[n // 2] for n in range(ATTN_CHAINS)]
        chain_keys = [(k1_ref, k2_ref)[n % 2] for n in range(ATTN_CHAINS)]
        qh = [q_ref[0, :, cs] for cs in chain_cols]
        for m_ref, l_ref, acc_ref in state:
            m_ref[...] = jnp.full(m_ref.shape, MASK_VALUE, F32)
            l_ref[...] = jnp.zeros(l_ref.shape, F32)
            acc_ref[...] = jnp.zeros(acc_ref.shape, F32)

        def scores(j, into):
            k0 = key_start(j)
            for n in range(ATTN_CHAINS):
                into[n][...] = _dot_nt(qh[n], chain_keys[n][0, pl.ds(k0, tk), chain_cols[n]])

        def block(j, cur, nxt):
            if nxt is not None:
                scores(j + 1, nxt)
            k0 = key_start(j)
            valid = (k_minus_q <= q0 - k0) & (kloc >= jnp.maximum(PAD, j * tk) - k0)
            for n in range(ATTN_CHAINS):
                m_ref, l_ref, acc_ref = state[n]
                s = jnp.where(valid, cur[n][...], MASK_VALUE)
                m_prev = m_ref[...]
                m_next = jnp.maximum(m_prev, jnp.max(s, axis=-1, keepdims=True))
                alpha = jnp.exp2(m_prev - m_next)
                p = jnp.exp2(s - jnp.concatenate([m_next] * nrep, axis=1))
                psum = p[:, 0:128]
                for r in range(1, nrep):
                    psum = psum + p[:, r * 128:(r + 1) * 128]
                m_ref[...] = m_next
                l_ref[...] = alpha * l_ref[...] + psum
                vb = v_ref[0, pl.ds(k0, tk), chain_cols[n]]
                acc_ref[...] = alpha * acc_ref[...] + _dot(p.astype(BF16), vb)

        def pair(t, carry):
            block(2 * t, s_a, s_b)
            block(2 * t + 1, s_b, s_a)
            return carry

        scores(0, s_a)
        lax.fori_loop(0, nkv // 2, pair, 0)

        @pl.when(nkv % 2 == 1)
        def _():
            block(nkv - 1, s_a, None)

        for hh, cs in enumerate(cols):
            (_, l1_ref, acc1_ref), (_, l2_ref, acc2_ref) = state[2 * hh], state[2 * hh + 1]
            l1 = jnp.sum(l1_ref[...], axis=-1, keepdims=True)
            l2 = jnp.sum(l2_ref[...], axis=-1, keepdims=True)
            o = acc1_ref[...] / l1 - lam * (acc2_ref[...] / l2)
            ms = jnp.mean(o * o, axis=-1, keepdims=True)
            o_ref[0, :, cs] = (o * lax.rsqrt(ms + RMS_EPS) * g * out_scale).astype(o_ref.dtype)


def _attn(q, k1, k2, v, lam_row, g_row, out_scale):
    bsz, lp, w = q.shape
    tq = _row_tile(lp, 544)
    tk = 256
    qs = pl.BlockSpec((1, tq, w), lambda b, i: (b, i, 0))
    full = pl.BlockSpec((1, lp, w), lambda b, i: (b, 0, 0))
    row = pl.BlockSpec((1, 128), lambda b, i: (0, 0))
    return pl.pallas_call(
        functools.partial(_attn_kernel, tq=tq, tk=tk, out_scale=out_scale),
        grid=(bsz, lp // tq),
        in_specs=[qs, full, full, full, row, row],
        out_specs=qs,
        out_shape=jax.ShapeDtypeStruct((bsz, lp, w), BF16),
        scratch_shapes=([pltpu.VMEM((tq, 128), F32)] * (3 * ATTN_CHAINS)
                        + [pltpu.VMEM((tq, tk), F32)] * (2 * ATTN_CHAINS)),
        compiler_params=_params(("parallel", "arbitrary")),
        name="diff_attn",
    )(q, k1, k2, v, lam_row, g_row)


def _router_logits(hn, wr_ref, br_ref):
    x_hi, x_lo = _split2(hn)
    both = _dot(x_hi, wr_ref[...])
    return (both[:, :ROUTER_LANES] + both[:, ROUTER_LANES:] + _dot(x_lo, wr_ref[:, :ROUTER_LANES])
            + br_ref[...])


def _router_group(logits):
    lane = lax.broadcasted_iota(jnp.int32, logits.shape, 1)
    is_g = lane < MOE_GROUPS
    glog = jnp.where(is_g, logits, MASK_VALUE)
    gmax = jnp.max(glog, axis=-1, keepdims=True)
    gsum = jnp.sum(jnp.where(is_g, jnp.exp(glog - gmax), 0.0), axis=-1, keepdims=True)
    gi = jnp.min(jnp.where(is_g & (glog == gmax), lane.astype(F32), 1e9), axis=-1, keepdims=True)
    return gi, 1.0 / gsum


def _router_gates(logits, gi, gp):
    lane = lax.broadcasted_iota(jnp.int32, logits.shape, 1)
    lanef = lane.astype(F32)
    big = 1e9
    e_idx = lane - MOE_GROUPS
    in_group = (e_idx >= 0) & (e_idx < MOE_EXPERTS) & ((e_idx >> 2) == gi)
    elog = jnp.where(in_group, logits, MASK_VALUE)
    v1 = jnp.max(elog, axis=-1, keepdims=True)
    i1 = jnp.min(jnp.where(in_group & (elog == v1), lanef, big), axis=-1, keepdims=True)
    rest = in_group & (lanef != i1)
    elog2 = jnp.where(rest, logits, MASK_VALUE)
    v2 = jnp.max(elog2, axis=-1, keepdims=True)
    i2 = jnp.min(jnp.where(rest & (elog2 == v2), lanef, big), axis=-1, keepdims=True)
    e21 = jnp.exp(v2 - v1)
    den = 1.0 / (1.0 + e21)
    return jnp.where(lanef == i1, gp * den, 0.0) + jnp.where(lanef == i2, gp * e21 * den, 0.0)


def _out_proj_kernel(h_ref, ys_ref, yr_ref, ya_ref, wo_ref, gn_ref, wr_ref, br_ref,
                     hout_ref, gi_ref, *, tl):
    acc = _dot(ys_ref[...].astype(BF16), wo_ref[0:256, :])
    acc += _dot(yr_ref[0].astype(BF16), wo_ref[256:512, :])
    acc += _dot(ya_ref[0], wo_ref[512:1024, :])
    h = _load_tokens(h_ref, tl) + acc
    rowpos = pl.program_id(1) * tl + lax.broadcasted_iota(jnp.int32, h.shape, 0)
    h = jnp.where(rowpos >= PAD, h, 0.0)
    _store_tokens(hout_ref, h)
    ms = jnp.mean(h * h, axis=-1, keepdims=True)
    hn = h * lax.rsqrt(ms + RMS_EPS) * gn_ref[...]
    gi, _ = _router_group(_router_logits(hn, wr_ref, br_ref))
    gi_ref[0] = gi.astype(jnp.int32)


def _out_proj(h_slab, bsz, lp, ys2d, yr, ya, wo_bf, gn, wr, br):
    d = SLAB * 128
    tl = _row_tile(lp, 1088)
    nl = lp // tl
    row = lambda b, i: (b, i, 0)
    const = lambda b, i: (0, 0)
    slab = pl.BlockSpec((tl * SLAB, 128), lambda b, i: (b * nl + i, 0))
    return pl.pallas_call(
        functools.partial(_out_proj_kernel, tl=tl),
        grid=(bsz, nl),
        in_specs=[slab,
                  pl.BlockSpec((tl, 256), lambda b, i: (i, b)),
                  pl.BlockSpec((1, tl, 256), row),
                  pl.BlockSpec((1, tl, 512), row),
                  pl.BlockSpec(wo_bf.shape, const),
                  pl.BlockSpec(gn.shape, const),
                  pl.BlockSpec(wr.shape, const),
                  pl.BlockSpec(br.shape, const)],
        out_specs=(slab, pl.BlockSpec((1, tl, 1), row)),
        out_shape=(jax.ShapeDtypeStruct((bsz * lp * SLAB, 128), F32),
                   jax.ShapeDtypeStruct((bsz, lp, 1), jnp.int32)),
        compiler_params=_params(("parallel", "parallel")),
        name="out_proj_router",
    )(h_slab, ys2d, yr, ya, wo_bf, gn, wr, br)


def _moe_kernel(grp_ref, used_ref, src_ref, srcn_ref, dst_ref, h_hbm, gn_ref, wr_ref, br_ref,
                wg_ref, wu_ref, wd_ref, out_hbm, xbuf, ybuf, gsem, ssem, *, tm):
    p = pl.program_id(0)
    used = used_ref[0]
    g = grp_ref[p]
    slot = p % 2

    def for_rows(start_row):
        def body(i, carry):
            for u in range(ROW_UNROLL):
                start_row(i * ROW_UNROLL + u, u % 2)
            return carry
        lax.fori_loop(0, tm // ROW_UNROLL, body, 0)

    def gather(idx_ref, into):
        def start_row(r, prio):
            rows = pl.ds(pl.multiple_of((into * tm + r) * SLAB, SLAB), SLAB)
            pltpu.make_async_copy(h_hbm.at[idx_ref[0, 0, r]], xbuf.at[rows, :],
                                  gsem.at[into]).start(priority=prio)
        for_rows(start_row)

    def scatter(dst_of_row):
        def start_row(r, prio):
            rows = pl.ds(pl.multiple_of(r * SLAB, SLAB), SLAB)
            pltpu.make_async_copy(ybuf.at[rows, :], out_hbm.at[dst_of_row(r)],
                                  ssem.at[0]).start(priority=prio)
        for_rows(start_row)

    def wait_tile(sem):
        pltpu.make_async_copy(h_hbm.at[pl.ds(0, tm)], out_hbm.at[pl.ds(0, tm)], sem).wait()

    @pl.when(p == 0)
    def _():
        gather(src_ref, 0)
        spare = out_hbm.shape[0] - tm
        ybuf[...] = jnp.zeros_like(ybuf)
        scatter(lambda r: spare + r)
        wait_tile(ssem.at[0])

    @pl.when(p < used)
    def _():
        wait_tile(gsem.at[slot])

        for s in range(2):
            @pl.when((p + 1 < used) & (slot == s))
            def _():
                gather(srcn_ref, 1 - s)

        h = _load_tokens(xbuf, tm, first=slot * tm)
        ms = jnp.mean(h * h, axis=-1, keepdims=True)
        hn = h * lax.rsqrt(ms + RMS_EPS) * gn_ref[...]
        logits = _router_logits(hn, wr_ref, br_ref)
        lane = lax.broadcasted_iota(jnp.int32, logits.shape, 1)
        is_g = lane < MOE_GROUPS
        glog = jnp.where(is_g, logits, MASK_VALUE)
        gmax = jnp.max(glog, axis=-1, keepdims=True)
        gsum = jnp.sum(jnp.where(is_g, jnp.exp(glog - gmax), 0.0), axis=-1, keepdims=True)
        lg = jnp.sum(jnp.where(lane == g, logits, 0.0), axis=-1, keepdims=True)
        gates = _router_gates(logits, g, jnp.exp(lg - gmax) / gsum)

        x = hn.astype(BF16)
        y = h
        for e in range(MOE_EXPERTS_PER_GROUP):
            ge = jnp.sum(jnp.where(lane == MOE_GROUPS + MOE_EXPERTS_PER_GROUP * g + e, gates, 0.0),
                         axis=-1, keepdims=True)
            a = _dot(x, wg_ref[e])
            b = _dot(x, wu_ref[e])
            he = (a * _sigmoid(a)) * b * ge
            y = y + _dot(he.astype(BF16), wd_ref[e])

        @pl.when(p > 0)
        def _():
            wait_tile(ssem.at[0])

        _store_tokens(ybuf, y)
        scatter(lambda r: dst_ref[0, 0, r])

        @pl.when(p + 1 == used)
        def _():
            wait_tile(ssem.at[0])


def _moe_routed(h_slab, gi, gn, wr, br, wg, wu, wd, tm=512):
    t = gi.shape[0]
    ne, d, ff = wg.shape
    tm = _row_tile(t, tm)
    nt = t // tm + MOE_GROUPS

    order = jnp.argsort(gi, stable=True).astype(jnp.int32)
    counts = jnp.sum((gi[:, None] == jnp.arange(MOE_GROUPS, dtype=jnp.int32)[None, :]).astype(jnp.int32),
                     axis=0)
    starts = jnp.cumsum(counts) - counts
    tiles_g = (counts + tm - 1) // tm
    tile_end = jnp.cumsum(tiles_g)
    tile_start = tile_end - tiles_g
    pidx = jnp.arange(nt, dtype=jnp.int32)
    grp = jnp.minimum(jnp.sum((pidx[:, None] >= tile_end[None, :]).astype(jnp.int32), axis=1),
                      MOE_GROUPS - 1)
    j = pidx - tile_start[grp]
    nval = jnp.where(pidx < tile_end[-1], jnp.clip(counts[grp] - j * tm, 0, tm), 0)
    lane = jnp.arange(tm, dtype=jnp.int32)[None, :]
    rows = starts[grp][:, None] + j[:, None] * tm + lane
    src = order[jnp.clip(rows, 0, t - 1)]
    dst = jnp.where(lane < nval[:, None], src, t + lane)
    src = src.reshape(nt, 1, tm)
    dst = dst.reshape(nt, 1, tm).astype(jnp.int32)
    used = tile_end[-1:].astype(jnp.int32)

    const2 = lambda p, grp, used: (0, 0)
    wspec = lambda shape: pl.BlockSpec((MOE_EXPERTS_PER_GROUP,) + shape, lambda p, grp, used: (grp[p], 0, 0))
    idx_spec = lambda shift: pl.BlockSpec(
        (1, 1, tm), lambda p, grp, used: (jnp.minimum(p + shift, nt - 1), 0, 0), memory_space=pltpu.SMEM)
    grid_spec = pltpu.PrefetchScalarGridSpec(
        num_scalar_prefetch=2,
        grid=(nt,),
        in_specs=[idx_spec(0), idx_spec(1), idx_spec(0),
                  pl.BlockSpec(memory_space=pl.ANY),
                  pl.BlockSpec(gn.shape, const2),
                  pl.BlockSpec(wr.shape, const2),
                  pl.BlockSpec(br.shape, const2),
                  wspec((d, ff)), wspec((d, ff)), wspec((ff, d))],
        out_specs=pl.BlockSpec(memory_space=pl.ANY),
        scratch_shapes=[pltpu.VMEM((2 * tm * SLAB, 128), F32), pltpu.VMEM((tm * SLAB, 128), F32),
                        pltpu.SemaphoreType.DMA((2,)), pltpu.SemaphoreType.DMA((1,))],
    )
    n_in = h_slab.shape[0] // SLAB
    out = pl.pallas_call(
        functools.partial(_moe_kernel, tm=tm),
        grid_spec=grid_spec,
        out_shape=jax.ShapeDtypeStruct((t + tm, SLAB, 128), F32),
        compiler_params=_params(("arbitrary",)),
        name="moe_experts",
    )(grp, used, src, src, dst, h_slab.reshape(n_in, SLAB, 128), gn, wr, br, wg, wu, wd)
    return out.reshape((t + tm) * SLAB, 128)


def _final_norm_kernel(h_ref, g_ref, o_ref, *, skip, chunk):
    g = g_ref[...]
    for c in range(o_ref.shape[1] // chunk):
        x = _load_tokens(h_ref, chunk, first=skip + c * chunk)
        ms = jnp.mean(x * x, axis=-1, keepdims=True)
        o_ref[0, c * chunk:(c + 1) * chunk, :] = x * lax.rsqrt(ms + RMS_EPS) * g


def _final_norm(h_slab, bsz, lp, g, seq):
    d = SLAB * 128
    chunk = _row_tile(seq, 512)
    return pl.pallas_call(
        functools.partial(_final_norm_kernel, skip=lp - seq, chunk=chunk),
        grid=(bsz,),
        in_specs=[pl.BlockSpec((lp * SLAB, 128), lambda b: (b, 0)),
                  pl.BlockSpec((1, d), lambda b: (0, 0))],
        out_specs=pl.BlockSpec((1, seq, d), lambda b: (b, 0, 0)),
        out_shape=jax.ShapeDtypeStruct((bsz, seq, d), F32),
        compiler_params=_params(("parallel",)),
        name="final_norm",
    )(h_slab, g.reshape(1, d))


def _s5_tables(lam_re, lam_im, log_dt, b_re, b_im, c_re, c_im):
    ng, ns = lam_re.shape
    lr = lam_re.astype(F32)
    li = lam_im.astype(F32)
    dt = jnp.exp(log_dt.astype(F32))[:, None]
    mag = jnp.exp(lr * dt)
    abar_r = mag * jnp.cos(li * dt)
    abar_i = mag * jnp.sin(li * dt)
    den = lr * lr + li * li
    zr = abar_r - 1.0
    zi = abar_i
    fr = (zr * lr + zi * li) / den
    fi = (zi * lr - zr * li) / den
    br = b_re.astype(F32)
    bi = b_im.astype(F32)
    bb_r = fr[..., None] * br - fi[..., None] * bi
    bb_i = fr[..., None] * bi + fi[..., None] * br
    eye = jnp.eye(ng, dtype=F32)
    nch = b_re.shape[2]
    to_b = lambda m: jnp.einsum('gnc,gh->gchn', m, eye).reshape(ng * nch, ng * ns)
    bbar = jnp.concatenate([to_b(bb_r), to_b(bb_i)], axis=1).astype(BF16)
    to_c = lambda m: jnp.einsum('gcn,gh->gnhc', m.astype(F32), eye).reshape(ng * ns, ng * nch)
    cmat = jnp.concatenate([to_c(c_re), -to_c(c_im)], axis=0).astype(BF16)
    a_row = jnp.concatenate([abar_r.reshape(1, -1), abar_i.reshape(1, -1)], axis=1)
    a8 = jnp.broadcast_to(a_row, (8, a_row.shape[1]))
    return bbar, a8, cmat


def _rope_tables(lp):
    half = 32
    inv_freq = 1.0 / (ROPE_THETA ** (jnp.arange(0, 2 * half, 2, dtype=F32) / (2 * half)))
    pos = jnp.arange(lp, dtype=F32) - float(PAD)
    ang = pos[:, None] * inv_freq[None, :]
    ang = jnp.concatenate([ang, ang, ang, ang], axis=-1)
    cos = jnp.cos(ang)
    sin = jnp.sin(ang)
    first = (jnp.arange(128) % 64) < half
    sina = jnp.where(first[None, :], -sin, 0.0)
    sinb = jnp.where(first[None, :], 0.0, sin)
    return cos, sina, sinb


def kernel(x, meta_tokens, norm_mix_g, w_in, s5_lambda_re, s5_lambda_im, s5_log_dt, s5_b_re, s5_b_im, s5_c_re, s5_c_im, s5_d, s5_w_glu, hgrn_lower_bounds, hgrn_norm_g, diff_lambda_q1, diff_lambda_k1, diff_lambda_q2, diff_lambda_k2, diff_subln_g, w_out, norm_ffn_g, moe_w_group, moe_b_group, moe_w_expert, moe_b_expert, moe_w_gate, moe_w_up, moe_w_down, final_norm_g):
    bsz, seq, d = x.shape
    depth = w_in.shape[0]
    lp = PAD + N_META + seq
    assert lp % 128 == 0 and bsz % 8 == 0

    assert d == SLAB * 128 and meta_tokens.shape[0] == N_META
    h = _embed(x.astype(F32), meta_tokens.astype(F32))

    cos, sina, sinb = _rope_tables(lp)
    lb_w = jax.nn.softmax(hgrn_lower_bounds.astype(F32), axis=0)
    lower_bounds = jnp.cumsum(lb_w, axis=0) - lb_w[0:1]

    for layer in range(depth):
        lam_init = 0.8 - 0.6 * math.exp(-0.3 * layer)
        u2d, hq, hf, hi, hg, aq, ak1, ak2, av = _in_proj(
            h, bsz, lp, norm_mix_g[layer].reshape(1, d).astype(F32), w_in[layer].astype(BF16), cos, sina, sinb)

        bbar, a8, cmat = _s5_tables(s5_lambda_re[layer], s5_lambda_im[layer], s5_log_dt[layer],
                                    s5_b_re[layer], s5_b_im[layer], s5_c_re[layer], s5_c_im[layer])
        y_ssm = _s5(u2d.reshape(lp, bsz, 256), bbar, a8, cmat,
                    s5_d[layer].reshape(1, -1).astype(F32), s5_w_glu[layer].astype(BF16))

        ng = jnp.tile(hgrn_norm_g[layer].astype(F32), HGRN_HEADS).reshape(1, -1)
        y_rec = _hgrn(hq, hf, hi, hg, lower_bounds[layer].reshape(1, -1), ng)

        lam = (jnp.exp(jnp.sum(diff_lambda_q1[layer].astype(F32) * diff_lambda_k1[layer].astype(F32)))
               - jnp.exp(jnp.sum(diff_lambda_q2[layer].astype(F32) * diff_lambda_k2[layer].astype(F32)))
               + lam_init)
        y_att = _attn(aq, ak1, ak2, av, jnp.full((1, 128), lam, F32),
                      diff_subln_g[layer].reshape(1, -1).astype(F32), 1.0 - lam_init)

        wr = jnp.concatenate([moe_w_group[layer].astype(F32), moe_w_expert[layer].astype(F32)], axis=1)
        wr = jnp.pad(wr, ((0, 0), (0, ROUTER_LANES - wr.shape[1])))
        wr_hi = wr.astype(BF16)
        wr = jnp.concatenate([wr_hi, (wr - wr_hi.astype(F32)).astype(BF16)], axis=1)
        br = jnp.concatenate([moe_b_group[layer].astype(F32), moe_b_expert[layer].astype(F32)])
        br = jnp.pad(br, (0, ROUTER_LANES - br.shape[0])).reshape(1, -1)
        gn = norm_ffn_g[layer].reshape(1, d).astype(F32)
        h, gi = _out_proj(h, bsz, lp, y_ssm.reshape(lp, bsz * 256), y_rec, y_att,
                          w_out[layer].astype(BF16), gn, wr, br)
        h = _moe_routed(h, gi.reshape(bsz * lp), gn, wr, br, moe_w_gate[layer].astype(BF16),
                        moe_w_up[layer].astype(BF16), moe_w_down[layer].astype(BF16))

    return _final_norm(h, bsz, lp, final_norm_g.astype(F32), seq)
```
